```python
import math
import jax
import jax.numpy as jnp
from jax import lax
import numpy as np

D_MODEL = 2048
BATCH = 4
SEQ = 2048
DEPTH = 4
DEC_BATCH = 8
DEC_SEQ = 4
PAST_LEN = 16384
PAGE_SIZE = 128

D_SSM = D_MODEL // 2
SSM_GROUP = 16
SSM_GROUPS = D_SSM // SSM_GROUP
SSM_STATE = 64
N_HEADS = 8
HEAD_DIM = 128
D_ATT = N_HEADS * HEAD_DIM
N_KV_HEADS = 4
GQA = N_HEADS // N_KV_HEADS
D_KV = N_KV_HEADS * HEAD_DIM
IDX_HEADS = 16
IDX_DIM = 64
TOPK_MAX = 256
Q_BLOCK = 128
N_MEM = 256
MEM_HEADS = 4
MEM_HEAD_DIM = 128
D_MEM = MEM_HEADS * MEM_HEAD_DIM
N_BRANCH = 3
ROPE_THETA = 10000.0
RMS_EPS = 1e-6
D_IN = 2 * D_SSM + 2 * D_ATT + 2 * D_KV + IDX_HEADS * IDX_DIM + IDX_DIM + IDX_HEADS + 2 * D_MEM + N_BRANCH * D_MODEL

kernel_name = 'hybrid_s5_dsa_memory_decoder_step'


def rmsnorm(x, g):
    xf = x.astype(jnp.float32)
    y = xf * lax.rsqrt(jnp.mean(xf * xf, axis=-1, keepdims=True) + RMS_EPS)
    return (y * g.astype(jnp.float32)).astype(x.dtype)


def rope(x, pos):
    half = x.shape[-1] // 2
    inv = ROPE_THETA ** (-jnp.arange(half, dtype=jnp.float32) / half)
    ang = pos[:, None] * inv[None, :]
    cos = jnp.cos(ang)[:, None, :]
    sin = jnp.sin(ang)[:, None, :]
    xf = x.astype(jnp.float32)
    x1, x2 = xf[..., :half], xf[..., half:]
    return jnp.concatenate([x1 * cos - x2 * sin, x1 * sin + x2 * cos], axis=-1).astype(x.dtype)


def split_in(h):
    sizes = (D_SSM, D_SSM, D_ATT, D_KV, D_KV, D_ATT, IDX_HEADS * IDX_DIM, IDX_DIM, IDX_HEADS,
             D_MEM, D_MEM, N_BRANCH * D_MODEL)
    parts = []
    start = 0
    for size in sizes:
        parts.append(h[..., start:start + size])
        start += size
    return parts


def project_inputs(x, g, w_in, pos):
    B, T, _ = x.shape
    h = rmsnorm(x, g)
    u, zs, q, k, v, za, iq, ik, iw, mq, zm, gates = split_in(h @ w_in)
    q = rope(q.reshape(B, T, N_HEADS, HEAD_DIM), pos)
    k = rope(k.reshape(B, T, N_KV_HEADS, HEAD_DIM), pos)
    v = v.reshape(B, T, N_KV_HEADS, HEAD_DIM)
    iq = rope(iq.reshape(B, T, IDX_HEADS, IDX_DIM), pos)
    ik = rope(ik.reshape(B, T, 1, IDX_DIM), pos)[:, :, 0]
    iw = iw * (IDX_HEADS * IDX_DIM) ** -0.5
    mq = mq.reshape(B, T, MEM_HEADS, MEM_HEAD_DIM)
    return u, zs, q, k, v, za, iq, ik, iw, mq, zm, gates


def _complex_combine(c1, c2):
    a1r, a1i, b1r, b1i = c1
    a2r, a2i, b2r, b2i = c2
    ar = a1r * a2r - a1i * a2i
    ai = a1r * a2i + a1i * a2r
    br = a2r * b1r - a2i * b1i + b2r
    bi = a2r * b1i + a2i * b1r + b2i
    return ar, ai, br, bi


def ssm_branch(u, s0_re, s0_im, a_re, a_im, log_dt, b_re, b_im, c_re, c_im, d, w_glu):
    Bt, T, _ = u.shape
    f32 = jnp.float32
    uf = u.astype(f32).reshape(Bt, T, SSM_GROUPS, SSM_GROUP)
    ar, ai = a_re.astype(f32), a_im.astype(f32)
    dt = jnp.exp(log_dt.astype(f32))[:, None]
    mag = jnp.exp(dt * ar)
    abar_r, abar_i = mag * jnp.cos(dt * ai), mag * jnp.sin(dt * ai)
    den = ar * ar + ai * ai
    nr, ni = abar_r - 1.0, abar_i
    coef_r = ((nr * ar + ni * ai) / den)[..., None]
    coef_i = ((ni * ar - nr * ai) / den)[..., None]
    br, bi = b_re.astype(f32), b_im.astype(f32)
    bbar_r = coef_r * br - coef_i * bi
    bbar_i = coef_r * bi + coef_i * br
    bu_r = jnp.einsum('gpc,btgc->btgp', bbar_r, uf)
    bu_i = jnp.einsum('gpc,btgc->btgp', bbar_i, uf)
    a_r = jnp.broadcast_to(abar_r, bu_r.shape)
    a_i = jnp.broadcast_to(abar_i, bu_i.shape)
    cum_r, cum_i, s_r, s_i = lax.associative_scan(_complex_combine, (a_r, a_i, bu_r, bu_i), axis=1)
    s0r = s0_re.astype(f32)[:, None]
    s0i = s0_im.astype(f32)[:, None]
    h_r = s_r + cum_r * s0r - cum_i * s0i
    h_i = s_i + cum_r * s0i + cum_i * s0r
    y = jnp.einsum('gcp,btgp->btgc', c_re.astype(f32), h_r) - jnp.einsum('gcp,btgp->btgc', c_im.astype(f32), h_i)
    y = y.reshape(Bt, T, D_SSM) + d.astype(f32) * u.astype(f32)
    y = jax.nn.gelu(y)
    proj = y @ w_glu.astype(f32)
    y = proj[..., :D_SSM] * jax.nn.sigmoid(proj[..., D_SSM:])
    return y.astype(u.dtype), h_r[:, -1], h_i[:, -1]


def indexer_scores(iq, iw, ik):
    s = jax.nn.relu(jnp.einsum('bqhd,bsd->bqhs', iq.astype(jnp.float32), ik.astype(jnp.float32)))
    return jnp.einsum('bqh,bqhs->bqs', iw.astype(jnp.float32), s)


def gather_rows(a, idx):
    return jax.vmap(lambda ab, ib: ab[ib])(a, idx)


def sparse_attend(q, kg, vg, valid):
    B, Q = q.shape[:2]
    qg = q.astype(jnp.float32).reshape(B, Q, N_KV_HEADS, GQA, HEAD_DIM)
    logits = jnp.einsum('bqhgd,bqnhd->bqhgn', qg, kg.astype(jnp.float32)) * HEAD_DIM ** -0.5
    logits = jnp.where(valid[:, :, None, None, :], logits, -jnp.inf)
    p = jax.nn.softmax(logits, axis=-1)
    o = jnp.einsum('bqhgn,bqnhd->bqhgd', p, vg.astype(jnp.float32))
    return o.reshape(B, Q, D_ATT).astype(q.dtype)


def dsa_prompt(q, k, v, iq, ik, iw):
    B, T = q.shape[:2]
    topk = min(TOPK_MAX, T // 4)
    nb = T // Q_BLOCK

    def to_blocks(a):
        return a.reshape((B, nb, Q_BLOCK) + a.shape[2:]).swapaxes(0, 1)

    starts = jnp.arange(nb, dtype=jnp.int32) * Q_BLOCK
    kpos = jnp.arange(T, dtype=jnp.int32)

    def block(args):
        qb, iqb, iwb, start = args
        qpos = start + jnp.arange(Q_BLOCK, dtype=jnp.int32)
        causal = kpos[None, None, :] <= qpos[None, :, None]
        sc = jnp.where(causal, indexer_scores(iqb, iwb, ik), -jnp.inf)
        _, idx = lax.top_k(sc, topk)
        valid = idx <= qpos[None, :, None]
        return sparse_attend(qb, gather_rows(k, idx), gather_rows(v, idx), valid)

    out = lax.map(block, (to_blocks(q), to_blocks(iq), to_blocks(iw), starts))
    return out.swapaxes(0, 1).reshape(B, T, D_ATT)


def dsa_sample(q, k, v, iq, ik, iw, cache_k, cache_v, cache_kidx, layer, page_table):
    DB, DS = q.shape[:2]
    L = PAST_LEN + DS
    topk = min(TOPK_MAX, L // 4)
    ik_past = cache_kidx[layer, page_table].reshape(DB, PAST_LEN, IDX_DIM)
    ik_all = jnp.concatenate([ik_past.astype(ik.dtype), ik], axis=1)
    qpos = PAST_LEN + jnp.arange(DS, dtype=jnp.int32)
    kpos = jnp.arange(L, dtype=jnp.int32)
    sc = jnp.where(kpos[None, None, :] <= qpos[None, :, None], indexer_scores(iq, iw, ik_all), -jnp.inf)
    _, idx = lax.top_k(sc, topk)
    in_past = (idx < PAST_LEN)[..., None, None]
    pidx = jnp.minimum(idx, PAST_LEN - 1)
    phys = page_table[jnp.arange(DB)[:, None, None], pidx // PAGE_SIZE]
    off = pidx % PAGE_SIZE
    nidx = jnp.clip(idx - PAST_LEN, 0, DS - 1)
    kg = jnp.where(in_past, cache_k[layer, phys, off].astype(k.dtype), gather_rows(k, nidx))
    vg = jnp.where(in_past, cache_v[layer, phys, off].astype(v.dtype), gather_rows(v, nidx))
    valid = idx <= qpos[None, :, None]
    return sparse_attend(q, kg, vg, valid)


def mem_kv(mem, g, w):
    B = mem.shape[0]
    kv = rmsnorm(mem, g) @ w
    mk = kv[..., :D_MEM].reshape(B, N_MEM, MEM_HEADS, MEM_HEAD_DIM)
    mv = kv[..., D_MEM:].reshape(B, N_MEM, MEM_HEADS, MEM_HEAD_DIM)
    return mk, mv


def mem_attend(mq, mk, mv):
    B, T = mq.shape[:2]
    logits = jnp.einsum('bqhd,bmhd->bhqm', mq.astype(jnp.float32), mk.astype(jnp.float32)) * MEM_HEAD_DIM ** -0.5
    p = jax.nn.softmax(logits, axis=-1)
    o = jnp.einsum('bhqm,bmhd->bqhd', p, mv.astype(jnp.float32))
    return o.reshape(B, T, D_MEM).astype(mq.dtype)


def merge_branches(x, y_ssm, zs, y_att, za, y_mem, zm, gates, w_out_ssm, w_out_att, w_out_mem, w_o):
    g_ssm, g_att, g_mem = jnp.split(jax.nn.sigmoid(gates), N_BRANCH, axis=-1)
    merged = (g_ssm * ((y_ssm * jax.nn.silu(zs)) @ w_out_ssm)
              + g_att * ((y_att * jax.nn.silu(za)) @ w_out_att)
              + g_mem * ((y_mem * jax.nn.silu(zm)) @ w_out_mem))
    return x + merged @ w_o


def setup_inputs(seed: int = 0) -> dict:
    key = jax.random.key(seed)
    ks = jax.random.split(key, 32)
    f32 = jnp.float32
    n_pages = PAST_LEN // PAGE_SIZE
    used = DEC_BATCH * n_pages
    pool = used + used // 4

    def nrm(k, shape, scale=1.0):
        return scale * jax.random.normal(k, shape, f32)

    page_table = jax.random.permutation(ks[0], pool)[:used].reshape(DEC_BATCH, n_pages).astype(jnp.int32)
    a_im = jnp.pi * jnp.arange(SSM_STATE, dtype=f32)[None, None, :] + nrm(ks[13], (DEPTH, SSM_GROUPS, SSM_STATE), 0.01)
    return {
        'x_prompt': nrm(ks[1], (BATCH, SEQ, D_MODEL)),
        'x_sample': nrm(ks[2], (DEC_BATCH, DEC_SEQ, D_MODEL)),
        'cache_k': nrm(ks[3], (DEPTH, pool, PAGE_SIZE, N_KV_HEADS, HEAD_DIM)),
        'cache_v': nrm(ks[4], (DEPTH, pool, PAGE_SIZE, N_KV_HEADS, HEAD_DIM)),
        'cache_kidx': nrm(ks[5], (DEPTH, pool, PAGE_SIZE, IDX_DIM)),
        'cache_mem_k': nrm(ks[6], (DEPTH, DEC_BATCH, N_MEM, MEM_HEADS, MEM_HEAD_DIM)),
        'cache_mem_v': nrm(ks[7], (DEPTH, DEC_BATCH, N_MEM, MEM_HEADS, MEM_HEAD_DIM)),
        'state_ssm_re': nrm(ks[8], (DEPTH, DEC_BATCH, SSM_GROUPS, SSM_STATE), 0.1),
        'state_ssm_im': nrm(ks[9], (DEPTH, DEC_BATCH, SSM_GROUPS, SSM_STATE), 0.1),
        'page_table': page_table,
        'mem_prompt': nrm(ks[10], (BATCH, N_MEM, D_MODEL)),
        'rms_g': 1.0 + nrm(ks[11], (DEPTH, D_MODEL), 0.02),
        'w_in': nrm(ks[12], (DEPTH, D_MODEL, D_IN), D_MODEL ** -0.5),
        'ssm_a_re': -0.5 + nrm(ks[14], (DEPTH, SSM_GROUPS, SSM_STATE), 0.01),
        'ssm_a_im': a_im,
        'ssm_log_dt': jax.random.uniform(ks[15], (DEPTH, SSM_GROUPS), f32, math.log(1e-3), math.log(1e-1)),
        'ssm_b_re': nrm(ks[16], (DEPTH, SSM_GROUPS, SSM_STATE, SSM_GROUP), (2 * SSM_GROUP) ** -0.5),
        'ssm_b_im': nrm(ks[17], (DEPTH, SSM_GROUPS, SSM_STATE, SSM_GROUP), (2 * SSM_GROUP) ** -0.5),
        'ssm_c_re': nrm(ks[18], (DEPTH, SSM_GROUPS, SSM_GROUP, SSM_STATE), 2.0 * SSM_STATE ** -0.5),
        'ssm_c_im': nrm(ks[19], (DEPTH, SSM_GROUPS, SSM_GROUP, SSM_STATE), 2.0 * SSM_STATE ** -0.5),
        'ssm_d': 1.0 + nrm(ks[20], (DEPTH, D_SSM), 0.1),
        'w_glu': nrm(ks[21], (DEPTH, D_SSM, 2 * D_SSM), D_SSM ** -0.5),
        'w_out_ssm': nrm(ks[22], (DEPTH, D_SSM, D_MODEL), D_SSM ** -0.5),
        'w_out_att': nrm(ks[23], (DEPTH, D_ATT, D_MODEL), D_ATT ** -0.5),
        'w_out_mem': nrm(ks[24], (DEPTH, D_MEM, D_MODEL), D_MEM ** -0.5),
        'mem_norm': 1.0 + nrm(ks[25], (DEPTH, D_MODEL), 0.02),
        'w_mem_kv': nrm(ks[26], (DEPTH, D_MODEL, 2 * D_MEM), D_MODEL ** -0.5),
        'w_o': nrm(ks[27], (DEPTH, D_MODEL, D_MODEL), D_MODEL ** -0.5),
        'final_norm': 1.0 + nrm(ks[28], (D_MODEL,), 0.02),
    }


def reference(x_prompt, x_sample, cache_k, cache_v, cache_kidx, cache_mem_k, cache_mem_v,
              state_ssm_re, state_ssm_im, page_table, mem_prompt, rms_g, w_in, ssm_a_re, ssm_a_im,
              ssm_log_dt, ssm_b_re, ssm_b_im, ssm_c_re, ssm_c_im, ssm_d, w_glu, w_out_ssm, w_out_att,
              w_out_mem, mem_norm, w_mem_kv, w_o, final_norm):
    Bp, T = x_prompt.shape[:2]
    DS = x_sample.shape[1]
    pos_p = jnp.arange(T, dtype=jnp.float32)
    pos_s = jnp.float32(PAST_LEN) + jnp.arange(DS, dtype=jnp.float32)
    xp, xs = x_prompt, x_sample
    kp_l, vp_l, ikp_l, mkp_l, mvp_l, srp_l, sip_l = [], [], [], [], [], [], []
    ks_l, vs_l, iks_l, srs_l, sis_l = [], [], [], [], []
    zero_state = jnp.zeros((Bp, SSM_GROUPS, SSM_STATE), jnp.float32)
    for l in range(DEPTH):
        ssm_p = (ssm_a_re[l], ssm_a_im[l], ssm_log_dt[l], ssm_b_re[l], ssm_b_im[l],
                 ssm_c_re[l], ssm_c_im[l], ssm_d[l], w_glu[l])
        outs = (w_out_ssm[l], w_out_att[l], w_out_mem[l], w_o[l])
        u, zs, q, k, v, za, iq, ik, iw, mq, zm, gates = project_inputs(xp, rms_g[l], w_in[l], pos_p)
        y_ssm, sr, si = ssm_branch(u, zero_state, zero_state, *ssm_p)
        y_att = dsa_prompt(q, k, v, iq, ik, iw)
        mk, mv = mem_kv(mem_prompt, mem_norm[l], w_mem_kv[l])
        y_mem = mem_attend(mq, mk, mv)
        xp = merge_branches(xp, y_ssm, zs, y_att, za, y_mem, zm, gates, *outs)
        kp_l.append(k); vp_l.append(v); ikp_l.append(ik)
        mkp_l.append(mk); mvp_l.append(mv); srp_l.append(sr); sip_l.append(si)
        u, zs, q, k, v, za, iq, ik, iw, mq, zm, gates = project_inputs(xs, rms_g[l], w_in[l], pos_s)
        y_ssm, sr, si = ssm_branch(u, state_ssm_re[l], state_ssm_im[l], *ssm_p)
        y_att = dsa_sample(q, k, v, iq, ik, iw, cache_k, cache_v, cache_kidx, l, page_table)
        y_mem = mem_attend(mq, cache_mem_k[l], cache_mem_v[l])
        xs = merge_branches(xs, y_ssm, zs, y_att, za, y_mem, zm, gates, *outs)
        ks_l.append(k); vs_l.append(v); iks_l.append(ik); srs_l.append(sr); sis_l.append(si)
    y_prompt = rmsnorm(xp, final_norm)
    y_sample = rmsnorm(xs, final_norm)
    return (y_prompt, y_sample,
            jnp.stack(kp_l), jnp.stack(vp_l), jnp.stack(ikp_l),
            jnp.stack(mkp_l), jnp.stack(mvp_l), jnp.stack(srp_l), jnp.stack(sip_l),
            jnp.stack(ks_l), jnp.stack(vs_l), jnp.stack(iks_l), jnp.stack(srs_l), jnp.stack(sis_l))
```

```python
import functools
import math

import jax
import jax.numpy as jnp
from jax import lax
from jax.experimental import pallas as pl
from jax.experimental.pallas import tpu as pltpu

F32 = jnp.float32
BF16 = jnp.bfloat16
I32 = jnp.int32

D_MODEL = 2048
D_SSM = 1024
SSM_GROUP = 16
SSM_GROUPS = 64
SSM_STATE = 64
N_HEADS = 8
HEAD_DIM = 128
D_ATT = 1024
N_KV_HEADS = 4
GQA = 2
D_KV = 512
IDX_HEADS = 16
IDX_DIM = 64
TOPK_MAX = 256
N_MEM = 256
MEM_HEADS = 4
MEM_HEAD_DIM = 128
D_MEM = 512
PAGE_SIZE = 128
ROPE_THETA = 10000.0
RMS_EPS = 1e-6
IW_SCALE = (IDX_HEADS * IDX_DIM) ** -0.5
ATT_SCALE = HEAD_DIM ** -0.5
MEM_SCALE = MEM_HEAD_DIM ** -0.5
NEG_BIG = -1e30
INT_MIN = -2 ** 31

LANES = 128
SUBLANES = 8
VMEM_LIMIT = 56 * 1024 * 1024

OFF_U, OFF_ZS, OFF_Q, OFF_ZA, OFF_IQ, OFF_G = 0, 1024, 2048, 3072, 4096, 5120
OFF_K, OFF_V, OFF_MQ, OFF_ZM = 11264, 11776, 12288, 12800
N_MAIN = 13312
N_STATE = 2 * SSM_GROUPS * SSM_STATE
N_STILE = N_STATE // LANES
N_RTILE = N_STILE // 2


def _cparams(*sem):
    return pltpu.CompilerParams(dimension_semantics=sem, vmem_limit_bytes=VMEM_LIMIT)


def _sigmoid(x):
    return 1.0 / (1.0 + jnp.exp(-x))


def _silu(x):
    return x * _sigmoid(x)


def _nt_dot(a, b):
    return lax.dot_general(a, b, (((1,), (1,)), ((), ())), preferred_element_type=F32)


def _rope128(acc, c, s):
    outs = []
    for k in range(acc.shape[1] // LANES):
        x = acc[:, k * LANES:(k + 1) * LANES]
        outs.append(x * c + pltpu.roll(x, 64, 1) * s)
    return outs[0] if len(outs) == 1 else jnp.concatenate(outs, axis=1)


def _rope64(acc, c, s):
    lane = lax.broadcasted_iota(I32, (1, LANES), 1)
    first = (lane % 64) < 32
    outs = []
    for k in range(acc.shape[1] // LANES):
        x = acc[:, k * LANES:(k + 1) * LANES]
        rot = jnp.where(first, pltpu.roll(x, 96, 1), pltpu.roll(x, 32, 1))
        outs.append(x * c + rot * s)
    return outs[0] if len(outs) == 1 else jnp.concatenate(outs, axis=1)


def _norm_matmul_kernel(x_ref, g_ref, w_ref, ca_ref, sa_ref, cb_ref, sb_ref, o_ref, xn_ref,
                        *, tiles128, tiles64):
    j = pl.program_id(1)

    @pl.when(j == 0)
    def _():
        x = x_ref[...]
        ms = jnp.mean(x * x, axis=-1, keepdims=True)
        xn_ref[...] = (x * lax.rsqrt(ms + RMS_EPS) * g_ref[...]).astype(BF16)

    acc = jnp.dot(xn_ref[...], w_ref[...], preferred_element_type=F32)

    def member(tiles):
        p = j == tiles[0]
        for t in tiles[1:]:
            p = p | (j == t)
        return p

    if not tiles128 and not tiles64:
        o_ref[...] = acc
        return
    plain = None
    if tiles128:
        p128 = member(tiles128)
        plain = jnp.logical_not(p128)

        @pl.when(p128)
        def _():
            o_ref[...] = _rope128(acc, ca_ref[...], sa_ref[...])
    if tiles64:
        p64 = member(tiles64)
        plain = jnp.logical_not(p64) if plain is None else plain & jnp.logical_not(p64)

        @pl.when(p64)
        def _():
            o_ref[...] = _rope64(acc, cb_ref[...], sb_ref[...])

    @pl.when(plain)
    def _():
        o_ref[...] = acc


def norm_matmul(x, g, w, tabs, *, tm, tn, tiles128=(), tiles64=()):
    M, K = x.shape
    N = w.shape[1]
    ca, sa, cb, sb = tabs
    nt = ca.shape[0] // tm
    tab_spec = pl.BlockSpec((tm, LANES), lambda i, j: (i % nt, 0))
    return pl.pallas_call(
        functools.partial(_norm_matmul_kernel, tiles128=tuple(tiles128), tiles64=tuple(tiles64)),
        grid=(M // tm, N // tn),
        in_specs=[
            pl.BlockSpec((tm, K), lambda i, j: (i, 0)),
            pl.BlockSpec((1, K), lambda i, j: (0, 0)),
            pl.BlockSpec((K, tn), lambda i, j: (0, j)),
            tab_spec, tab_spec, tab_spec, tab_spec,
        ],
        out_specs=pl.BlockSpec((tm, tn), lambda i, j: (i, j)),
        out_shape=jax.ShapeDtypeStruct((M, N), F32),
        scratch_shapes=[pltpu.VMEM((tm, K), BF16)],
        compiler_params=_cparams("parallel", "arbitrary"),
        name="norm_matmul",
    )(x, g.reshape(1, K), w, ca, sa, cb, sb)


def _ssm_kernel(u_ref, zs_ref, s0_ref, a_ref, wb_ref, wc_ref, d_ref, wglu_ref, o_ref, sn_ref,
                bh_ref, st_ref, *, nb, tc, lt):
    j = pl.program_id(0)
    rows = nb * tc

    @pl.when(j == 0)
    def _():
        st_ref[...] = s0_ref[...]

    u = u_ref[...]
    ub = u.astype(BF16)
    for n in range(8):
        res = jnp.dot(ub[:, n * LANES:(n + 1) * LANES], wb_ref[n], preferred_element_type=F32)
        for k in range(4):
            bh_ref[n * 4 + k] = res[:, k * LANES:(k + 1) * LANES]
            bh_ref[N_RTILE + n * 4 + k] = res[:, 512 + k * LANES:512 + (k + 1) * LANES]

    sub = lax.broadcasted_iota(I32, (SUBLANES, LANES), 0)
    low = sub < 4
    for c in range(N_RTILE // lt):
        tr = [c * lt + k for k in range(lt)]
        ti = [N_RTILE + c * lt + k for k in range(lt)]
        ar = [a_ref[t] for t in tr]
        ai = [a_ref[t] for t in ti]
        h0 = tuple(st_ref[t] for t in tr), tuple(st_ref[t] for t in ti)

        def cstep(hr, hi, k, br, bi):
            return ar[k] * hr - ai[k] * hi + br, ar[k] * hi + ai[k] * hr + bi

        if nb == SUBLANES:
            def step(t, carry):
                hr, hi = carry
                off = pl.multiple_of(t * SUBLANES, SUBLANES)
                nr, ni = [], []
                for k in range(lt):
                    r, i = cstep(hr[k], hi[k], k, bh_ref[tr[k], pl.ds(off, SUBLANES), :],
                                 bh_ref[ti[k], pl.ds(off, SUBLANES), :])
                    bh_ref[tr[k], pl.ds(off, SUBLANES), :] = r
                    bh_ref[ti[k], pl.ds(off, SUBLANES), :] = i
                    nr.append(r)
                    ni.append(i)
                return tuple(nr), tuple(ni)

            hN = lax.fori_loop(0, tc, step, h0)
        else:
            def step(t, carry):
                hr, hi = carry
                off = pl.multiple_of(t * SUBLANES, SUBLANES)
                nr, ni = [], []
                for k in range(lt):
                    br = bh_ref[tr[k], pl.ds(off, SUBLANES), :]
                    bi = bh_ref[ti[k], pl.ds(off, SUBLANES), :]
                    r1, i1 = cstep(hr[k], hi[k], k, br, bi)
                    r2, i2 = cstep(pltpu.roll(r1, 4, 0), pltpu.roll(i1, 4, 0), k, br, bi)
                    bh_ref[tr[k], pl.ds(off, SUBLANES), :] = jnp.where(low, r1, r2)
                    bh_ref[ti[k], pl.ds(off, SUBLANES), :] = jnp.where(low, i1, i2)
                    nr.append(pltpu.roll(r2, 4, 0))
                    ni.append(pltpu.roll(i2, 4, 0))
                return tuple(nr), tuple(ni)

            hN = lax.fori_loop(0, rows // SUBLANES, step, h0)
        for k in range(lt):
            st_ref[tr[k]] = hN[0][k]
            st_ref[ti[k]] = hN[1][k]

    sn_ref[...] = st_ref[...]

    ys = []
    for n in range(8):
        hcat = jnp.concatenate([bh_ref[n * 4 + k] for k in range(4)]
                               + [bh_ref[N_RTILE + n * 4 + k] for k in range(4)], axis=1)
        ys.append(jnp.dot(hcat.astype(BF16), wc_ref[n], preferred_element_type=F32))
    y = jnp.concatenate(ys, axis=1) + d_ref[...] * u
    y = jax.nn.gelu(y, approximate=True)
    proj = jnp.dot(y.astype(BF16), wglu_ref[...], preferred_element_type=F32)
    y = proj[:, :D_SSM] * _sigmoid(proj[:, D_SSM:])
    o_ref[...] = (y * _silu(zs_ref[...])).astype(BF16)


def ssm_branch(h, s0, a_t, wb, wc, d, wglu, *, nb, tc):
    M = h.shape[0]
    rows = nb * tc
    const3 = lambda j: (0, 0, 0)
    return pl.pallas_call(
        functools.partial(_ssm_kernel, nb=nb, tc=tc, lt=4),
        grid=(M // rows,),
        in_specs=[
            pl.BlockSpec((rows, D_SSM), lambda j: (j, OFF_U // D_SSM)),
            pl.BlockSpec((rows, D_SSM), lambda j: (j, OFF_ZS // D_SSM)),
            pl.BlockSpec((N_STILE, SUBLANES, LANES), const3),
            pl.BlockSpec((N_STILE, SUBLANES, LANES), const3),
            pl.BlockSpec((8, LANES, 1024), const3),
            pl.BlockSpec((8, 1024, LANES), const3),
            pl.BlockSpec((1, D_SSM), lambda j: (0, 0)),
            pl.BlockSpec((D_SSM, 2 * D_SSM), lambda j: (0, 0)),
        ],
        out_specs=[
            pl.BlockSpec((rows, D_SSM), lambda j: (j, 0)),
            pl.BlockSpec((N_STILE, SUBLANES, LANES), const3),
        ],
        out_shape=[
            jax.ShapeDtypeStruct((M, D_SSM), BF16),
            jax.ShapeDtypeStruct((N_STILE, SUBLANES, LANES), F32),
        ],
        scratch_shapes=[
            pltpu.VMEM((N_STILE, rows, LANES), F32),
            pltpu.VMEM((N_STILE, SUBLANES, LANES), F32),
        ],
        compiler_params=_cparams("arbitrary"),
        name="ssm_branch",
    )(h, h, s0, a_t, wb, wc, d.reshape(1, D_SSM), wglu)


def _sort_key(score):
    bits = pltpu.bitcast(score, I32)
    return jnp.where(bits < 0, bits ^ jnp.int32(0x7FFFFFFF), bits)


def _kth_largest_key(count_ge, shape, k):
    thr = jnp.full(shape, INT_MIN, I32)
    for bit in range(31, -1, -1):
        inc = INT_MIN if bit == 31 else (1 << bit)
        cand = thr + jnp.int32(inc)
        thr = jnp.where(count_ge(cand) >= k, cand, thr)
    return thr


def _dsa_prompt_kernel(q_ref, za_ref, iq_ref, iwq_ref, k_ref, v_ref, ik_ref, o_ref, key_ref,
                       *, tq, topk):
    qi = pl.program_id(1)
    T = k_ref.shape[0]
    ik = ik_ref[:, 0:IDX_DIM].astype(BF16)
    iq = iq_ref[...].astype(BF16)
    iw = iwq_ref[:, IDX_DIM:IDX_DIM + IDX_HEADS] * IW_SCALE
    sc = jnp.zeros((tq, T), F32)
    for h in range(IDX_HEADS):
        s = _nt_dot(iq[:, h * IDX_DIM:(h + 1) * IDX_DIM], ik)
        sc = sc + jnp.maximum(s, 0.0) * iw[:, h:h + 1]
    qpos = qi * tq + lax.broadcasted_iota(I32, (tq, 1), 0)
    kpos = lax.broadcasted_iota(I32, (1, T), 1)
    causal = kpos <= qpos
    key_ref[...] = _sort_key(jnp.where(causal, sc, -jnp.inf))

    def count_ge(cand):
        return jnp.sum((key_ref[...] >= cand).astype(F32), axis=-1, keepdims=True)

    thr = _kth_largest_key(count_ge, (tq, 1), float(topk))
    sel = (key_ref[...] >= thr) & causal

    for kvh in range(N_KV_HEADS):
        kh = k_ref[:, kvh * HEAD_DIM:(kvh + 1) * HEAD_DIM].astype(BF16)
        vh = v_ref[:, kvh * HEAD_DIM:(kvh + 1) * HEAD_DIM].astype(BF16)
        for g in range(GQA):
            cs = slice((kvh * GQA + g) * HEAD_DIM, (kvh * GQA + g + 1) * HEAD_DIM)
            lg = _nt_dot(q_ref[:, cs].astype(BF16), kh) * ATT_SCALE
            lg = jnp.where(sel, lg, NEG_BIG)
            m = jnp.max(lg, axis=-1, keepdims=True)
            p = jnp.exp(lg - m)
            l = jnp.sum(p, axis=-1, keepdims=True)
            o = jnp.dot(p.astype(BF16), vh, preferred_element_type=F32) / l
            o_ref[:, cs] = (o * _silu(za_ref[:, cs])).astype(BF16)


def dsa_prompt(h2, ikw2, *, nb, tq):
    T = h2.shape[0]
    topk = min(TOPK_MAX, T // 4)
    nq = N_MAIN // D_ATT
    nk = N_MAIN // D_KV
    return pl.pallas_call(
        functools.partial(_dsa_prompt_kernel, tq=tq, topk=topk),
        grid=(nb, T // tq),
        in_specs=[
            pl.BlockSpec((tq, D_ATT), lambda b, i: (i, b * nq + OFF_Q // D_ATT)),
            pl.BlockSpec((tq, D_ATT), lambda b, i: (i, b * nq + OFF_ZA // D_ATT)),
            pl.BlockSpec((tq, D_ATT), lambda b, i: (i, b * nq + OFF_IQ // D_ATT)),
            pl.BlockSpec((tq, LANES), lambda b, i: (i, b)),
            pl.BlockSpec((T, D_KV), lambda b, i: (0, b * nk + OFF_K // D_KV)),
            pl.BlockSpec((T, D_KV), lambda b, i: (0, b * nk + OFF_V // D_KV)),
            pl.BlockSpec((T, LANES), lambda b, i: (0, b)),
        ],
        out_specs=pl.BlockSpec((tq, D_ATT), lambda b, i: (i, b)),
        out_shape=jax.ShapeDtypeStruct((T, nb * D_ATT), BF16),
        scratch_shapes=[pltpu.VMEM((tq, T), I32)],
        compiler_params=_cparams("parallel", "arbitrary"),
        name="dsa_prompt",
    )(h2, h2, h2, ikw2, h2, h2, ikw2)


def _mem_attn_kernel(q_ref, z_ref, k_ref, v_ref, o_ref):
    for hh in range(MEM_HEADS):
        cs = slice(hh * MEM_HEAD_DIM, (hh + 1) * MEM_HEAD_DIM)
        lg = _nt_dot(q_ref[:, cs].astype(BF16), k_ref[:, cs].astype(BF16)) * MEM_SCALE
        m = jnp.max(lg, axis=-1, keepdims=True)
        p = jnp.exp(lg - m)
        l = jnp.sum(p, axis=-1, keepdims=True)
        o = jnp.dot(p.astype(BF16), v_ref[:, cs].astype(BF16), preferred_element_type=F32) / l
        o_ref[:, cs] = (o * _silu(z_ref[:, cs])).astype(BF16)


def mem_attn_prompt(h2, mkv, *, nb, tq):
    T = h2.shape[0]
    nm = N_MAIN // D_MEM
    return pl.pallas_call(
        _mem_attn_kernel,
        grid=(nb, T // tq),
        in_specs=[
            pl.BlockSpec((tq, D_MEM), lambda b, i: (i, b * nm + OFF_MQ // D_MEM)),
            pl.BlockSpec((tq, D_MEM), lambda b, i: (i, b * nm + OFF_ZM // D_MEM)),
            pl.BlockSpec((N_MEM, D_MEM), lambda b, i: (b, 0)),
            pl.BlockSpec((N_MEM, D_MEM), lambda b, i: (b, 1)),
        ],
        out_specs=pl.BlockSpec((tq, D_MEM), lambda b, i: (i, b)),
        out_shape=jax.ShapeDtypeStruct((T, nb * D_MEM), BF16),
        compiler_params=_cparams("parallel", "arbitrary"),
        name="mem_attn_prompt",
    )(h2, h2, mkv, mkv)


def mem_attn_sample(mq, zm, mk, mv):
    B, R, _ = mq.shape
    qspec = pl.BlockSpec((None, R, D_MEM), lambda b: (b, 0, 0))
    kspec = pl.BlockSpec((None, N_MEM, D_MEM), lambda b: (b, 0, 0))
    return pl.pallas_call(
        _mem_attn_kernel,
        grid=(B,),
        in_specs=[qspec, qspec, kspec, kspec],
        out_specs=qspec,
        out_shape=jax.ShapeDtypeStruct((B, R, D_MEM), BF16),
        compiler_params=_cparams("parallel"),
        name="mem_attn_sample",
    )(mq, zm, mk, mv)


def _merge_kernel(as_ref, aa_ref, am_ref, gs_ref, ga_ref, gm_ref, ws_ref, wa_ref, wm_ref, o_ref):
    ys = jnp.dot(as_ref[...], ws_ref[...], preferred_element_type=F32)
    ya = jnp.dot(aa_ref[...], wa_ref[...], preferred_element_type=F32)
    ym = jnp.dot(am_ref[...], wm_ref[...], preferred_element_type=F32)
    o = _sigmoid(gs_ref[...]) * ys + _sigmoid(ga_ref[...]) * ya + _sigmoid(gm_ref[...]) * ym
    o_ref[...] = o.astype(BF16)


def merge_branches(a_ssm, a_att, a_mem, h, w_ssm, w_att, w_mem, *, tm, tn=1024):
    M = h.shape[0]
    gb = OFF_G // tn
    per = D_MODEL // tn
    return pl.pallas_call(
        _merge_kernel,
        grid=(M // tm, per),
        in_specs=[
            pl.BlockSpec((tm, D_SSM), lambda i, n: (i, 0)),
            pl.BlockSpec((tm, D_ATT), lambda i, n: (i, 0)),
            pl.BlockSpec((tm, D_MEM), lambda i, n: (i, 0)),
            pl.BlockSpec((tm, tn), lambda i, n: (i, gb + n)),
            pl.BlockSpec((tm, tn), lambda i, n: (i, gb + per + n)),
            pl.BlockSpec((tm, tn), lambda i, n: (i, gb + 2 * per + n)),
            pl.BlockSpec((D_SSM, tn), lambda i, n: (0, n)),
            pl.BlockSpec((D_ATT, tn), lambda i, n: (0, n)),
            pl.BlockSpec((D_MEM, tn), lambda i, n: (0, n)),
        ],
        out_specs=pl.BlockSpec((tm, tn), lambda i, n: (i, n)),
        out_shape=jax.ShapeDtypeStruct((M, D_MODEL), BF16),
        compiler_params=_cparams("parallel", "arbitrary"),
        name="merge_branches",
    )(a_ssm, a_att, a_mem, h, h, h, w_ssm, w_att, w_mem)


def _out_kernel(m_ref, w_ref, x_ref, g_ref, o_ref, *, final_norm):
    y = x_ref[...] + jnp.dot(m_ref[...], w_ref[...], preferred_element_type=F32)
    if final_norm:
        ms = jnp.mean(y * y, axis=-1, keepdims=True)
        y = y * lax.rsqrt(ms + RMS_EPS) * g_ref[...]
    o_ref[...] = y


def out_proj(merged, w_o, x, g, *, tm, final_norm):
    M = x.shape[0]
    return pl.pallas_call(
        functools.partial(_out_kernel, final_norm=final_norm),
        grid=(M // tm,),
        in_specs=[
            pl.BlockSpec((tm, D_MODEL), lambda i: (i, 0)),
            pl.BlockSpec((D_MODEL, D_MODEL), lambda i: (0, 0)),
            pl.BlockSpec((tm, D_MODEL), lambda i: (i, 0)),
            pl.BlockSpec((1, D_MODEL), lambda i: (0, 0)),
        ],
        out_specs=pl.BlockSpec((tm, D_MODEL), lambda i: (i, 0)),
        out_shape=jax.ShapeDtypeStruct((M, D_MODEL), F32),
        compiler_params=_cparams("parallel"),
        name="out_proj",
    )(merged, w_o, x, g.reshape(1, D_MODEL))


def _page_scores(iq_ref, wsel_ref, page):
    s = _nt_dot(iq_ref[...], page.astype(BF16))
    return jnp.dot(wsel_ref[...], jnp.maximum(s, 0.0), preferred_element_type=F32,
                   precision=lax.Precision.HIGHEST)


def _idx_sample_kernel(pt_ref, iq_ref, wsel_ref, *rest, pp):
    pages, o_ref = rest[:pp], rest[pp]
    for i in range(pp):
        o_ref[:, i * PAGE_SIZE:(i + 1) * PAGE_SIZE] = _page_scores(iq_ref, wsel_ref, pages[i][...])


def idx_scores_sample(cache_kidx, layer, pt_flat, iqm, wsel, *, n_pages, pp):
    B = iqm.shape[0]

    def page_spec(i):
        return pl.BlockSpec((None, None, PAGE_SIZE, IDX_DIM),
                            lambda b, s, pt: (layer, pt[b * n_pages + s * pp + i], 0, 0))

    grid_spec = pltpu.PrefetchScalarGridSpec(
        num_scalar_prefetch=1,
        grid=(B, n_pages // pp),
        in_specs=[
            pl.BlockSpec((None,) + iqm.shape[1:], lambda b, s, pt: (b, 0, 0)),
            pl.BlockSpec((None,) + wsel.shape[1:], lambda b, s, pt: (b, 0, 0)),
        ] + [page_spec(i) for i in range(pp)],
        out_specs=pl.BlockSpec((None, SUBLANES, pp * PAGE_SIZE), lambda b, s, pt: (b, 0, s)),
    )
    return pl.pallas_call(
        functools.partial(_idx_sample_kernel, pp=pp),
        grid_spec=grid_spec,
        out_shape=jax.ShapeDtypeStruct((B, SUBLANES, n_pages * PAGE_SIZE), F32),
        compiler_params=_cparams("parallel", "arbitrary"),
        name="idx_scores_sample",
    )(pt_flat, iqm, wsel, *([cache_kidx] * pp))


def _attn_sample_kernel(pt_ref, sc_ref, iq_ref, wsel_ref, iknew_ref, q_ref, knew_ref, vnew_ref, *rest,
                        pp, ds, topk):
    kpages, vpages = rest[:pp], rest[pp:2 * pp]
    o_ref = rest[2 * pp]
    key_ref, keyn_ref, thr_ref, m_ref, l_ref, acc_ref = rest[2 * pp + 1:]
    s = pl.program_id(1)
    ns = pl.num_programs(1)

    @pl.when(s == 0)
    def _():
        sn = _page_scores(iq_ref, wsel_ref, iknew_ref[...])
        row = lax.broadcasted_iota(I32, sn.shape, 0) % ds
        col = lax.broadcasted_iota(I32, sn.shape, 1)
        keyn_ref[...] = _sort_key(jnp.where(col <= row, sn, -jnp.inf))
        key_ref[...] = _sort_key(sc_ref[...])

        def count_ge(cand):
            c = jnp.sum((key_ref[...] >= cand).astype(F32), axis=-1, keepdims=True)
            return c + jnp.sum((keyn_ref[...] >= cand).astype(F32), axis=-1, keepdims=True)

        thr = _kth_largest_key(count_ge, (SUBLANES, 1), float(topk))
        thr_ref[...] = jnp.broadcast_to(thr, thr_ref.shape)
        m_ref[...] = jnp.full(m_ref.shape, NEG_BIG, F32)
        l_ref[...] = jnp.zeros(l_ref.shape, F32)
        acc_ref[...] = jnp.zeros(acc_ref.shape, F32)

    thr = thr_ref[...]

    def update(kvh, lg, vs):
        m_old = m_ref[kvh]
        m_new = jnp.maximum(m_old, jnp.max(lg, axis=-1, keepdims=True))
        alpha = jnp.exp(m_old - m_new)
        p = jnp.exp(lg - m_new)
        l_ref[kvh] = alpha * l_ref[kvh] + jnp.sum(p, axis=-1, keepdims=True)
        pb = p.astype(BF16)
        pv = jnp.dot(pb[:, 0:PAGE_SIZE], vs[0], preferred_element_type=F32)
        for i in range(1, len(vs)):
            pv = pv + jnp.dot(pb[:, i * PAGE_SIZE:(i + 1) * PAGE_SIZE], vs[i], preferred_element_type=F32)
        acc_ref[kvh] = alpha * acc_ref[kvh] + pv
        m_ref[kvh] = m_new

    base = pl.multiple_of(s * (pp * PAGE_SIZE), pp * PAGE_SIZE)
    sel = key_ref[:, pl.ds(base, pp * PAGE_SIZE)] >= thr[:, 0:1]
    for kvh in range(N_KV_HEADS):
        cs = slice(kvh * HEAD_DIM, (kvh + 1) * HEAD_DIM)
        qh = q_ref[kvh]
        lg = jnp.concatenate([_nt_dot(qh, kpages[i][:, cs].astype(BF16)) for i in range(pp)], axis=1)
        lg = jnp.where(sel, lg * ATT_SCALE, NEG_BIG)
        update(kvh, lg, [vpages[i][:, cs].astype(BF16) for i in range(pp)])

    @pl.when(s == ns - 1)
    def _():
        seln = keyn_ref[...] >= thr[:, 0:1]
        for kvh in range(N_KV_HEADS):
            cs = slice(kvh * HEAD_DIM, (kvh + 1) * HEAD_DIM)
            lg = _nt_dot(q_ref[kvh], knew_ref[:, cs].astype(BF16)) * ATT_SCALE
            update(kvh, jnp.where(seln, lg, NEG_BIG), [vnew_ref[:, cs].astype(BF16)])
            o_ref[kvh] = acc_ref[kvh] / l_ref[kvh]


def attn_sample(cache_k, cache_v, layer, pt_flat, scores, iqm, wsel, iknew, qm, knew, vnew,
                *, n_pages, pp, ds):
    B = qm.shape[0]
    L = n_pages * PAGE_SIZE
    topk = min(TOPK_MAX, (L + ds) // 4)

    def page_spec(i):
        return pl.BlockSpec((None, None, PAGE_SIZE, D_KV),
                            lambda b, s, pt: (layer, pt[b * n_pages + s * pp + i], 0, 0))

    def bspec(a):
        nd = a.ndim - 1
        return pl.BlockSpec((None,) + a.shape[1:], lambda b, s, pt: (b,) + (0,) * nd)

    grid_spec = pltpu.PrefetchScalarGridSpec(
        num_scalar_prefetch=1,
        grid=(B, n_pages // pp),
        in_specs=[bspec(scores), bspec(iqm), bspec(wsel), bspec(iknew), bspec(qm), bspec(knew), bspec(vnew)]
        + [page_spec(i) for i in range(pp)] * 2,
        out_specs=pl.BlockSpec((None, N_KV_HEADS, SUBLANES, HEAD_DIM), lambda b, s, pt: (b, 0, 0, 0)),
        scratch_shapes=[
            pltpu.VMEM((SUBLANES, L), I32),
            pltpu.VMEM((SUBLANES, LANES), I32),
            pltpu.VMEM((SUBLANES, LANES), I32),
            pltpu.VMEM((N_KV_HEADS, SUBLANES, 1), F32),
            pltpu.VMEM((N_KV_HEADS, SUBLANES, 1), F32),
            pltpu.VMEM((N_KV_HEADS, SUBLANES, HEAD_DIM), F32),
        ],
    )
    return pl.pallas_call(
        functools.partial(_attn_sample_kernel, pp=pp, ds=ds, topk=topk),
        grid_spec=grid_spec,
        out_shape=jax.ShapeDtypeStruct((B, N_KV_HEADS, SUBLANES, HEAD_DIM), F32),
        compiler_params=_cparams("parallel", "arbitrary"),
        name="attn_sample",
    )(pt_flat, scores, iqm, wsel, iknew, qm, knew, vnew, *([cache_k] * pp), *([cache_v] * pp))


def _ssm_disc_kernel(ar_ref, ai_ref, ldt_ref, abr_ref, abi_ref, cr_ref, ci_ref):
    ar, ai = ar_ref[...], ai_ref[...]
    dt = jnp.exp(ldt_ref[...])
    mag = jnp.exp(dt * ar)
    abr = mag * jnp.cos(dt * ai)
    abi = mag * jnp.sin(dt * ai)
    den = ar * ar + ai * ai
    nr, ni = abr - 1.0, abi
    abr_ref[...] = abr
    abi_ref[...] = abi
    cr_ref[...] = (nr * ar + ni * ai) / den
    ci_ref[...] = (ni * ar - nr * ai) / den


def ssm_discretise(a_re, a_im, log_dt):
    G, P = a_re.shape
    ldt = jnp.broadcast_to(log_dt[:, None], (G, P))
    shp = jax.ShapeDtypeStruct((G, P), F32)
    return pl.pallas_call(_ssm_disc_kernel, out_shape=[shp] * 4, name="ssm_discretise")(a_re, a_im, ldt)


def _bbar_kernel(cr_ref, ci_ref, br_ref, bi_ref, or_ref, oi_ref):
    cr, ci, br, bi = cr_ref[...], ci_ref[...], br_ref[...], bi_ref[...]
    or_ref[...] = cr * br - ci * bi
    oi_ref[...] = cr * bi + ci * br


def ssm_bbar(coef_r, coef_i, b_re, b_im):
    G, P, C = b_re.shape
    cr = jnp.broadcast_to(coef_r[:, :, None], (G, P, C)).reshape(G, P * C)
    ci = jnp.broadcast_to(coef_i[:, :, None], (G, P, C)).reshape(G, P * C)
    shp = jax.ShapeDtypeStruct((G, P * C), F32)
    r, i = pl.pallas_call(_bbar_kernel, out_shape=[shp] * 2, name="ssm_bbar")(
        cr, ci, b_re.reshape(G, P * C), b_im.reshape(G, P * C))
    return r.reshape(G, P, C), i.reshape(G, P, C)


def _pack_w_in(w):
    o = [0, 1024, 2048, 3072, 3584, 4096, 5120, 6144, 6208, 6224, 6736, 7248, 13392]
    u, zs, q, k, v, za, iq, ik, iw, mq, zm, gates = [w[:, o[i]:o[i + 1]] for i in range(12)]
    main = jnp.concatenate([u, zs, q, za, iq, gates, k, v, mq, zm], axis=1).astype(BF16)
    pad = jnp.zeros((w.shape[0], LANES - IDX_DIM - IDX_HEADS), w.dtype)
    ikw = jnp.concatenate([ik, iw, pad], axis=1).astype(BF16)
    return main, ikw


def _rope_tables(pos):
    def tabs(half):
        inv = ROPE_THETA ** (-jnp.arange(half, dtype=F32) / half)
        ang = pos[:, None] * inv[None, :]
        return jnp.cos(ang), jnp.sin(ang)

    c, s = tabs(64)
    c128 = jnp.concatenate([c, c], axis=1)
    s128 = jnp.concatenate([-s, s], axis=1)
    c, s = tabs(32)
    c64 = jnp.concatenate([c, c, c, c], axis=1)
    s64 = jnp.concatenate([-s, s, -s, s], axis=1)
    lane = jnp.arange(LANES)[None, :]
    c64k = jnp.where(lane < IDX_DIM, c64, 1.0)
    s64k = jnp.where(lane < IDX_DIM, s64, 0.0)
    return (c128, s128, c64, s64), (c128, s128, c64k, s64k)


def _ssm_weights(abr, abi, bbar_r, bbar_i, c_re, c_im):
    a_t = jnp.concatenate([abr.reshape(N_RTILE, 1, LANES), abi.reshape(N_RTILE, 1, LANES)], axis=0)
    a_t = jnp.broadcast_to(a_t, (N_STILE, SUBLANES, LANES))
    eye = jnp.eye(8, dtype=F32)

    def bmat(bb):
        bb = bb.reshape(8, 8, SSM_STATE, SSM_GROUP)
        return jnp.einsum('ngpc,hg->nhcgp', bb, eye).reshape(8, LANES, 512)

    wb = jnp.concatenate([bmat(bbar_r), bmat(bbar_i)], axis=2).astype(BF16)

    def cmat(cc):
        cc = cc.reshape(8, 8, SSM_GROUP, SSM_STATE)
        return jnp.einsum('ngcp,hg->nhpgc', cc, eye).reshape(8, 512, LANES)

    wc = jnp.concatenate([cmat(c_re), -cmat(c_im)], axis=1).astype(BF16)
    return a_t, wb, wc


def _state_to_tiles(s_re, s_im):
    nb = s_re.shape[0]

    def t(s):
        s = s.reshape(nb, N_RTILE, LANES).transpose(1, 0, 2)
        return jnp.pad(s, ((0, 0), (0, SUBLANES - nb), (0, 0)))

    return jnp.concatenate([t(s_re), t(s_im)], axis=0)


def _tiles_to_state(st, nb):
    def t(s):
        return s[:, :nb].transpose(1, 0, 2).reshape(nb, SSM_GROUPS, SSM_STATE)

    return t(st[:N_RTILE]), t(st[N_RTILE:])


def _project(x, g, w_main, w_ikw, tabs_main, tabs_ikw, tm):
    h = norm_matmul(x, g, w_main, tabs_main, tm=tm, tn=512,
                    tiles128=(4, 5, 22), tiles64=(8, 9))
    ikw = norm_matmul(x, g, w_ikw, tabs_ikw, tm=tm, tn=LANES, tiles64=(0,))
    return h, ikw


def kernel(x_prompt, x_sample, cache_k, cache_v, cache_kidx, cache_mem_k, cache_mem_v, state_ssm_re, state_ssm_im, page_table, mem_prompt, rms_g, w_in, ssm_a_re, ssm_a_im, ssm_log_dt, ssm_b_re, ssm_b_im, ssm_c_re, ssm_c_im, ssm_d, w_glu, w_out_ssm, w_out_att, w_out_mem, mem_norm, w_mem_kv, w_o, final_norm):
    Bp, T, _ = x_prompt.shape
    DB, DS, _ = x_sample.shape
    depth = w_in.shape[0]
    n_pages = page_table.shape[1]
    past = n_pages * PAGE_SIZE
    pool = cache_k.shape[1]

    xp = x_prompt.transpose(1, 0, 2).reshape(T * Bp, D_MODEL)
    xs = x_sample.transpose(1, 0, 2).reshape(DS * DB, D_MODEL)
    pos_p = jnp.repeat(jnp.arange(T, dtype=F32), Bp)
    pos_s = jnp.repeat(jnp.float32(past) + jnp.arange(DS, dtype=F32), DB)
    tabs_p, tabs_pk = _rope_tables(pos_p)
    tabs_s, tabs_sk = _rope_tables(pos_s)
    mem_rows = mem_prompt.reshape(Bp * N_MEM, D_MODEL)
    one_tab = jnp.ones((Bp * N_MEM, LANES), F32)
    pt_flat = page_table.reshape(-1).astype(I32)
    cache_k4 = cache_k.reshape(depth, pool, PAGE_SIZE, D_KV)
    cache_v4 = cache_v.reshape(depth, pool, PAGE_SIZE, D_KV)
    zero_state = jnp.zeros((Bp, SSM_GROUPS, SSM_STATE), F32)

    kp_l, vp_l, ikp_l, mkp_l, mvp_l, srp_l, sip_l = [], [], [], [], [], [], []
    ks_l, vs_l, iks_l, srs_l, sis_l = [], [], [], [], []
    for l in range(depth):
        w_main, w_ikw = _pack_w_in(w_in[l])
        abr, abi, coef_r, coef_i = ssm_discretise(ssm_a_re[l], ssm_a_im[l], ssm_log_dt[l])
        bbar_r, bbar_i = ssm_bbar(coef_r, coef_i, ssm_b_re[l], ssm_b_im[l])
        a_t, wb, wc = _ssm_weights(abr, abi, bbar_r, bbar_i, ssm_c_re[l], ssm_c_im[l])
        wglu = w_glu[l].astype(BF16)
        w_ssm, w_att, w_mem = (w_out_ssm[l].astype(BF16), w_out_att[l].astype(BF16),
                               w_out_mem[l].astype(BF16))
        wo = w_o[l].astype(BF16)
        last = l == depth - 1

        h, ikw = _project(xp, rms_g[l], w_main, w_ikw, tabs_p, tabs_pk, tm=1024)
        a_ssm, st = ssm_branch(h, _state_to_tiles(zero_state, zero_state), a_t, wb, wc, ssm_d[l], wglu,
                               nb=Bp, tc=64)
        h2 = h.reshape(T, Bp * N_MAIN)
        ikw2 = ikw.reshape(T, Bp * LANES)
        a_att = dsa_prompt(h2, ikw2, nb=Bp, tq=128).reshape(T * Bp, D_ATT)
        mkv = norm_matmul(mem_rows, mem_norm[l], w_mem_kv[l].astype(BF16), (one_tab,) * 4,
                          tm=Bp * N_MEM, tn=512)
        a_mem = mem_attn_prompt(h2, mkv, nb=Bp, tq=512).reshape(T * Bp, D_MEM)
        merged = merge_branches(a_ssm, a_att, a_mem, h, w_ssm, w_att, w_mem, tm=512)
        xp = out_proj(merged, wo, xp, final_norm, tm=512, final_norm=last)
        sr, si = _tiles_to_state(st, Bp)
        kp_l.append(h[:, OFF_K:OFF_K + D_KV].reshape(T, Bp, N_KV_HEADS, HEAD_DIM).transpose(1, 0, 2, 3))
        vp_l.append(h[:, OFF_V:OFF_V + D_KV].reshape(T, Bp, N_KV_HEADS, HEAD_DIM).transpose(1, 0, 2, 3))
        ikp_l.append(ikw[:, :IDX_DIM].reshape(T, Bp, IDX_DIM).transpose(1, 0, 2))
        mkp_l.append(mkv[:, :D_MEM].reshape(Bp, N_MEM, MEM_HEADS, MEM_HEAD_DIM))
        mvp_l.append(mkv[:, D_MEM:].reshape(Bp, N_MEM, MEM_HEADS, MEM_HEAD_DIM))
        srp_l.append(sr)
        sip_l.append(si)

        h, ikw = _project(xs, rms_g[l], w_main, w_ikw, tabs_s, tabs_sk, tm=DS * DB)
        a_ssm, st = ssm_branch(h, _state_to_tiles(state_ssm_re[l], state_ssm_im[l]), a_t, wb, wc,
                               ssm_d[l], wglu, nb=DB, tc=DS)
        hb = h.reshape(DS, DB, N_MAIN).transpose(1, 0, 2)
        ikwb = ikw.reshape(DS, DB, LANES).transpose(1, 0, 2)
        iqm = hb[:, :, OFF_IQ:OFF_IQ + IDX_HEADS * IDX_DIM].reshape(DB, DS * IDX_HEADS, IDX_DIM).astype(BF16)
        iw = ikwb[:, :, IDX_DIM:IDX_DIM + IDX_HEADS] * IW_SCALE
        wsel = jnp.einsum('bth,ts->btsh', iw, jnp.eye(DS, dtype=F32)).reshape(DB, DS, DS * IDX_HEADS)
        wsel = jnp.concatenate([wsel] * (SUBLANES // DS), axis=1)
        rpad = ((0, 0), (0, PAGE_SIZE - DS), (0, 0))
        iknew = jnp.pad(ikwb[:, :, :IDX_DIM], rpad)
        knew = jnp.pad(hb[:, :, OFF_K:OFF_K + D_KV], rpad)
        vnew = jnp.pad(hb[:, :, OFF_V:OFF_V + D_KV], rpad)
        qm = hb[:, :, OFF_Q:OFF_Q + D_ATT].reshape(DB, DS, N_KV_HEADS, GQA, HEAD_DIM)
        qm = qm.transpose(0, 2, 3, 1, 4).reshape(DB, N_KV_HEADS, GQA * DS, HEAD_DIM).astype(BF16)
        scores = idx_scores_sample(cache_kidx, l, pt_flat, iqm, wsel, n_pages=n_pages, pp=8)
        o = attn_sample(cache_k4, cache_v4, l, pt_flat, scores, iqm, wsel, iknew, qm, knew, vnew,
                        n_pages=n_pages, pp=8, ds=DS)
        y_att = o.reshape(DB, N_KV_HEADS, GQA, DS, HEAD_DIM).transpose(3, 0, 1, 2, 4).reshape(DS * DB, D_ATT)
        a_att = (y_att * _silu(h[:, OFF_ZA:OFF_ZA + D_ATT])).astype(BF16)
        tpad = ((0, 0), (0, SUBLANES - DS), (0, 0))
        a_mem = mem_attn_sample(jnp.pad(hb[:, :, OFF_MQ:OFF_MQ + D_MEM], tpad),
                                jnp.pad(hb[:, :, OFF_ZM:OFF_ZM + D_MEM], tpad),
                                cache_mem_k[l].reshape(DB, N_MEM, D_MEM),
                                cache_mem_v[l].reshape(DB, N_MEM, D_MEM))
        a_mem = a_mem[:, :DS].transpose(1, 0, 2).reshape(DS * DB, D_MEM)
        merged = merge_branches(a_ssm, a_att, a_mem, h, w_ssm, w_att, w_mem, tm=DS * DB)
        xs = out_proj(merged, wo, xs, final_norm, tm=DS * DB, final_norm=last)
        sr, si = _tiles_to_state(st, DB)
        ks_l.append(hb[:, :, OFF_K:OFF_K + D_KV].reshape(DB, DS, N_KV_HEADS, HEAD_DIM))
        vs_l.append(hb[:, :, OFF_V:OFF_V + D_KV].reshape(DB, DS, N_KV_HEADS, HEAD_DIM))
        iks_l.append(ikwb[:, :, :IDX_DIM])
        srs_l.append(sr)
        sis_l.append(si)

    y_prompt = xp.reshape(T, Bp, D_MODEL).transpose(1, 0, 2)
    y_sample = xs.reshape(DS, DB, D_MODEL).transpose(1, 0, 2)
    return (y_prompt, y_sample,
            jnp.stack(kp_l), jnp.stack(vp_l), jnp.stack(ikp_l),
            jnp.stack(mkp_l), jnp.stack(mvp_l), jnp.stack(srp_l), jnp.stack(sip_l),
            jnp.stack(ks_l), jnp.stack(vs_l), jnp.stack(iks_l), jnp.stack(srs_l), jnp.stack(sis_l))
```

```python
import functools
import math

import jax
import jax.numpy as jnp
from jax import lax
from jax.experimental import pallas as pl
from jax.experimental.pallas import tpu as pltpu

F32 = jnp.float32
BF16 = jnp.bfloat16
I32 = jnp.int32

D_MODEL = 2048
D_SSM = 1024
SSM_GROUP = 16
SSM_GROUPS = 64
SSM_STATE = 64
N_HEADS = 8
HEAD_DIM = 128
D_ATT = 1024
N_KV_HEADS = 4
GQA = 2
D_KV = 512
IDX_HEADS = 16
IDX_DIM = 64
TOPK_MAX = 256
N_MEM = 256
MEM_HEADS = 4
MEM_HEAD_DIM = 128
D_MEM = 512
PAGE_SIZE = 128
ROPE_THETA = 10000.0
RMS_EPS = 1e-6
IW_SCALE = (IDX_HEADS * IDX_DIM) ** -0.5
ATT_SCALE = HEAD_DIM ** -0.5
MEM_SCALE = MEM_HEAD_DIM ** -0.5
LOG2E = 1.0 / math.log(2.0)
NEG_BIG = -1e30
INT_MIN = -2 ** 31

LANES = 128
SUBLANES = 8
VMEM_LIMIT = 56 * 1024 * 1024

OFF_U, OFF_ZS, OFF_Q, OFF_ZA, OFF_IQ, OFF_G = 0, 1024, 2048, 3072, 4096, 5120
OFF_K, OFF_V, OFF_MQ, OFF_ZM = 11264, 11776, 12288, 12800
N_MAIN = 13312
N_STATE = 2 * SSM_GROUPS * SSM_STATE
N_STILE = N_STATE // LANES
N_RTILE = N_STILE // 2


def _cparams(*sem):
    return pltpu.CompilerParams(dimension_semantics=sem, vmem_limit_bytes=VMEM_LIMIT)


def _sigmoid(x):
    return 0.5 * jnp.tanh(0.5 * x) + 0.5


def _silu(x):
    return x * _sigmoid(x)


def _nt_dot(a, b):
    return lax.dot_general(a, b, (((1,), (1,)), ((), ())), preferred_element_type=F32)


def _rope128(acc, c, s):
    outs = []
    for k in range(acc.shape[1] // LANES):
        x = acc[:, k * LANES:(k + 1) * LANES]
        outs.append(x * c + pltpu.roll(x, 64, 1) * s)
    return outs[0] if len(outs) == 1 else jnp.concatenate(outs, axis=1)


def _rope64(acc, c, s):
    lane = lax.broadcasted_iota(I32, (1, LANES), 1)
    first = (lane % 64) < 32
    outs = []
    for k in range(acc.shape[1] // LANES):
        x = acc[:, k * LANES:(k + 1) * LANES]
        rot = jnp.where(first, pltpu.roll(x, 96, 1), pltpu.roll(x, 32, 1))
        outs.append(x * c + rot * s)
    return outs[0] if len(outs) == 1 else jnp.concatenate(outs, axis=1)


def _norm_matmul_kernel(x_ref, g_ref, w_ref, ca_ref, sa_ref, cb_ref, sb_ref, o_ref, xn_ref,
                        *, tiles128, tiles64):
    j = pl.program_id(1)

    @pl.when(j == 0)
    def _():
        x = x_ref[...]
        ms = jnp.mean(x * x, axis=-1, keepdims=True)
        xn_ref[...] = (x * lax.rsqrt(ms + RMS_EPS) * g_ref[...]).astype(BF16)

    def matmul():
        return jnp.dot(xn_ref[...], w_ref[...], preferred_element_type=F32)

    plain = None
    for tile, ncols in tiles128 + tiles64:
        hit = j == tile
        plain = jnp.logical_not(hit) if plain is None else plain & jnp.logical_not(hit)

        @pl.when(hit)
        def _(tile=tile, ncols=ncols):
            acc = matmul()
            if (tile, ncols) in tiles128:
                o_ref[:, :ncols] = _rope128(acc[:, :ncols], ca_ref[...], sa_ref[...])
            else:
                o_ref[:, :ncols] = _rope64(acc[:, :ncols], cb_ref[...], sb_ref[...])
            if ncols < acc.shape[1]:
                o_ref[:, ncols:] = acc[:, ncols:]

    if plain is None:
        o_ref[...] = matmul()
    else:
        @pl.when(plain)
        def _():
            o_ref[...] = matmul()


def norm_matmul(x, g, w, tabs, *, tm, tn, tiles128=(), tiles64=()):
    M, K = x.shape
    N = w.shape[1]
    ca, sa, cb, sb = tabs
    nt = ca.shape[0] // tm
    tab_spec = pl.BlockSpec((tm, LANES), lambda i, j: (i % nt, 0))
    return pl.pallas_call(
        functools.partial(_norm_matmul_kernel, tiles128=tuple(tiles128), tiles64=tuple(tiles64)),
        grid=(M // tm, N // tn),
        in_specs=[
            pl.BlockSpec((tm, K), lambda i, j: (i, 0)),
            pl.BlockSpec((1, K), lambda i, j: (0, 0)),
            pl.BlockSpec((K, tn), lambda i, j: (0, j)),
            tab_spec, tab_spec, tab_spec, tab_spec,
        ],
        out_specs=pl.BlockSpec((tm, tn), lambda i, j: (i, j)),
        out_shape=jax.ShapeDtypeStruct((M, N), F32),
        scratch_shapes=[pltpu.VMEM((tm, K), BF16)],
        compiler_params=_cparams("parallel", "arbitrary"),
        name="norm_matmul",
    )(x, g.reshape(1, K), w, ca, sa, cb, sb)


def _ssm_kernel(u_ref, zs_ref, s0_ref, a_ref, wb_ref, wc_ref, d_ref, wglu_ref, o_ref, sn_ref,
                bh_ref, st_ref, *, nb, tc, lt):
    j = pl.program_id(0)
    rows = nb * tc

    @pl.when(j == 0)
    def _():
        st_ref[...] = s0_ref[...]

    u = u_ref[...].reshape(rows, D_SSM)
    ri = lax.broadcasted_iota(I32, (rows, 1), 0)
    ci = lax.broadcasted_iota(I32, (1, rows), 1)
    to_tb = (ci == (ri % nb) * tc + ri // nb).astype(BF16)
    to_bt = (ci == (ri % tc) * nb + ri // tc).astype(BF16)
    ub = jnp.dot(to_tb, u.astype(BF16), preferred_element_type=F32).astype(BF16)
    for n in range(8):
        res = jnp.dot(ub[:, n * LANES:(n + 1) * LANES], wb_ref[n], preferred_element_type=F32)
        for k in range(4):
            bh_ref[n * 4 + k] = res[:, k * LANES:(k + 1) * LANES]
            bh_ref[N_RTILE + n * 4 + k] = res[:, 512 + k * LANES:512 + (k + 1) * LANES]

    sub = lax.broadcasted_iota(I32, (SUBLANES, LANES), 0)
    low = sub < 4
    for c in range(N_RTILE // lt):
        tr = [c * lt + k for k in range(lt)]
        ti = [N_RTILE + c * lt + k for k in range(lt)]
        ar = [a_ref[t] for t in tr]
        ai = [a_ref[t] for t in ti]
        h0 = tuple(st_ref[t] for t in tr), tuple(st_ref[t] for t in ti)

        def cstep(hr, hi, k, br, bi):
            return ar[k] * hr - ai[k] * hi + br, ar[k] * hi + ai[k] * hr + bi

        if nb == SUBLANES:
            def step(t, carry):
                hr, hi = carry
                off = pl.multiple_of(t * SUBLANES, SUBLANES)
                nr, ni = [], []
                for k in range(lt):
                    r, i = cstep(hr[k], hi[k], k, bh_ref[tr[k], pl.ds(off, SUBLANES), :],
                                 bh_ref[ti[k], pl.ds(off, SUBLANES), :])
                    bh_ref[tr[k], pl.ds(off, SUBLANES), :] = r
                    bh_ref[ti[k], pl.ds(off, SUBLANES), :] = i
                    nr.append(r)
                    ni.append(i)
                return tuple(nr), tuple(ni)

            hN = lax.fori_loop(0, tc, step, h0)
        else:
            def step(t, carry):
                hr, hi = carry
                off = pl.multiple_of(t * SUBLANES, SUBLANES)
                nr, ni = [], []
                for k in range(lt):
                    br = bh_ref[tr[k], pl.ds(off, SUBLANES), :]
                    bi = bh_ref[ti[k], pl.ds(off, SUBLANES), :]
                    r1, i1 = cstep(hr[k], hi[k], k, br, bi)
                    r2, i2 = cstep(pltpu.roll(r1, 4, 0), pltpu.roll(i1, 4, 0), k, br, bi)
                    bh_ref[tr[k], pl.ds(off, SUBLANES), :] = jnp.where(low, r1, r2)
                    bh_ref[ti[k], pl.ds(off, SUBLANES), :] = jnp.where(low, i1, i2)
                    nr.append(pltpu.roll(r2, 4, 0))
                    ni.append(pltpu.roll(i2, 4, 0))
                return tuple(nr), tuple(ni)

            hN = lax.fori_loop(0, rows // SUBLANES, step, h0)
        for k in range(lt):
            st_ref[tr[k]] = hN[0][k]
            st_ref[ti[k]] = hN[1][k]

    sn_ref[...] = st_ref[...]

    ys = []
    for n in range(8):
        hcat = jnp.concatenate([bh_ref[n * 4 + k] for k in range(4)]
                               + [bh_ref[N_RTILE + n * 4 + k] for k in range(4)], axis=1)
        ys.append(jnp.dot(hcat.astype(BF16), wc_ref[n], preferred_element_type=F32))
    yc = jnp.concatenate(ys, axis=1)
    y_hi = yc.astype(BF16)
    rem = yc - y_hi.astype(F32)
    y_mid = rem.astype(BF16)
    y_lo = (rem - y_mid.astype(F32)).astype(BF16)
    yc = (jnp.dot(to_bt, y_hi, preferred_element_type=F32)
          + jnp.dot(to_bt, y_mid, preferred_element_type=F32)
          + jnp.dot(to_bt, y_lo, preferred_element_type=F32))
    y = yc + d_ref[...] * u
    y = jax.nn.gelu(y, approximate=True)
    proj = jnp.dot(y.astype(BF16), wglu_ref[...], preferred_element_type=F32)
    y = proj[:, :D_SSM] * _sigmoid(proj[:, D_SSM:])
    out = (y * _silu(zs_ref[...].reshape(rows, D_SSM))).astype(BF16)
    o_ref[...] = out.reshape(o_ref.shape)


def ssm_branch(h, s0, a_t, wb, wc, d, wglu, *, nb, tc):
    M = h.shape[0]
    T = M // nb
    rows = nb * tc
    const3 = lambda j: (0, 0, 0)
    if T == tc:
        hv = h
        u_spec = pl.BlockSpec((rows, D_SSM), lambda j: (0, OFF_U // D_SSM))
        z_spec = pl.BlockSpec((rows, D_SSM), lambda j: (0, OFF_ZS // D_SSM))
        o_spec = pl.BlockSpec((rows, D_SSM), lambda j: (0, 0))
        o_shape = jax.ShapeDtypeStruct((M, D_SSM), BF16)
    else:
        hv = h.reshape(nb, T, h.shape[1])
        u_spec = pl.BlockSpec((nb, tc, D_SSM), lambda j: (0, j, OFF_U // D_SSM))
        z_spec = pl.BlockSpec((nb, tc, D_SSM), lambda j: (0, j, OFF_ZS // D_SSM))
        o_spec = pl.BlockSpec((nb, tc, D_SSM), lambda j: (0, j, 0))
        o_shape = jax.ShapeDtypeStruct((nb, T, D_SSM), BF16)
    out, st = pl.pallas_call(
        functools.partial(_ssm_kernel, nb=nb, tc=tc, lt=4),
        grid=(T // tc,),
        in_specs=[
            u_spec,
            z_spec,
            pl.BlockSpec((N_STILE, SUBLANES, LANES), const3),
            pl.BlockSpec((N_STILE, SUBLANES, LANES), const3),
            pl.BlockSpec((8, LANES, 1024), const3),
            pl.BlockSpec((8, 1024, LANES), const3),
            pl.BlockSpec((1, D_SSM), lambda j: (0, 0)),
            pl.BlockSpec((D_SSM, 2 * D_SSM), lambda j: (0, 0)),
        ],
        out_specs=[
            o_spec,
            pl.BlockSpec((N_STILE, SUBLANES, LANES), const3),
        ],
        out_shape=[
            o_shape,
            jax.ShapeDtypeStruct((N_STILE, SUBLANES, LANES), F32),
        ],
        scratch_shapes=[
            pltpu.VMEM((N_STILE, rows, LANES), F32),
            pltpu.VMEM((N_STILE, SUBLANES, LANES), F32),
        ],
        compiler_params=_cparams("arbitrary"),
        name="ssm_branch",
    )(hv, hv, s0, a_t, wb, wc, d.reshape(1, D_SSM), wglu)
    return out.reshape(M, D_SSM), st


def _sort_key(score):
    bits = pltpu.bitcast(score, I32)
    return jnp.where(bits < 0, bits ^ jnp.int32(0x7FFFFFFF), bits)


def _kth_largest_key(count_ge, shape, k):
    thr = jnp.full(shape, INT_MIN, I32)
    for bit in range(31, -1, -1):
        inc = INT_MIN if bit == 31 else (1 << bit)
        cand = thr + jnp.int32(inc)
        thr = jnp.where(count_ge(cand) >= k, cand, thr)
    return thr


def _dsa_prompt_body(q_ref, za_ref, iq_ref, iwq_ref, k_ref, v_ref, ik_ref, o_ref, key_ref, qi,
                     *, tq, topk, nk):
    ik = ik_ref[0:nk, 0:IDX_DIM].astype(BF16)
    iq = iq_ref[...].astype(BF16)
    iw = iwq_ref[:, IDX_DIM:IDX_DIM + IDX_HEADS] * IW_SCALE
    sc = jnp.zeros((tq, nk), F32)
    for h in range(IDX_HEADS):
        s = _nt_dot(iq[:, h * IDX_DIM:(h + 1) * IDX_DIM], ik)
        sc = sc + jnp.maximum(s, 0.0) * iw[:, h:h + 1]
    qpos = qi * tq + lax.broadcasted_iota(I32, (tq, 1), 0)
    kpos = lax.broadcasted_iota(I32, (1, nk), 1)
    causal = kpos <= qpos
    key_ref[:, 0:nk] = _sort_key(jnp.where(causal, sc, -jnp.inf))

    def count_ge(cand):
        return jnp.sum((key_ref[:, 0:nk] >= cand).astype(F32), axis=-1, keepdims=True)

    thr = _kth_largest_key(count_ge, (tq, 1), float(topk))
    sel = (key_ref[:, 0:nk] >= thr) & causal

    for kvh in range(N_KV_HEADS):
        kh = k_ref[0:nk, kvh * HEAD_DIM:(kvh + 1) * HEAD_DIM].astype(BF16)
        vh = v_ref[0:nk, kvh * HEAD_DIM:(kvh + 1) * HEAD_DIM].astype(BF16)
        for g in range(GQA):
            cs = slice((kvh * GQA + g) * HEAD_DIM, (kvh * GQA + g + 1) * HEAD_DIM)
            lg = jnp.where(sel, _nt_dot(q_ref[:, cs].astype(BF16), kh), NEG_BIG)
            m = jnp.max(lg, axis=-1, keepdims=True)
            p = jnp.exp2((lg - m) * (ATT_SCALE * LOG2E))
            l = jnp.sum(p, axis=-1, keepdims=True)
            o = jnp.dot(p.astype(BF16), vh, preferred_element_type=F32) / l
            o_ref[:, cs] = (o * _silu(za_ref[:, cs])).astype(BF16)


def _dsa_prompt_kernel(q_ref, za_ref, iq_ref, iwq_ref, k_ref, v_ref, ik_ref, o_ref, key_ref,
                       *, tq, topk, kc):
    qi = pl.program_id(1)
    T = k_ref.shape[0]
    need = (qi + 1) * tq
    for v in range(1, T // kc + 1):
        @pl.when((need > (v - 1) * kc) & (need <= v * kc))
        def _(v=v):
            _dsa_prompt_body(q_ref, za_ref, iq_ref, iwq_ref, k_ref, v_ref, ik_ref, o_ref, key_ref, qi,
                             tq=tq, topk=topk, nk=v * kc)


def dsa_prompt(h, ikw, *, nb, tq, kc=512):
    T = h.shape[0] // nb
    topk = min(TOPK_MAX, T // 4)
    nq = T // tq
    return pl.pallas_call(
        functools.partial(_dsa_prompt_kernel, tq=tq, topk=topk, kc=kc),
        grid=(nb, nq),
        in_specs=[
            pl.BlockSpec((tq, D_ATT), lambda b, i: (b * nq + i, OFF_Q // D_ATT)),
            pl.BlockSpec((tq, D_ATT), lambda b, i: (b * nq + i, OFF_ZA // D_ATT)),
            pl.BlockSpec((tq, D_ATT), lambda b, i: (b * nq + i, OFF_IQ // D_ATT)),
            pl.BlockSpec((tq, LANES), lambda b, i: (b * nq + i, 0)),
            pl.BlockSpec((T, D_KV), lambda b, i: (b, OFF_K // D_KV)),
            pl.BlockSpec((T, D_KV), lambda b, i: (b, OFF_V // D_KV)),
            pl.BlockSpec((T, LANES), lambda b, i: (b, 0)),
        ],
        out_specs=pl.BlockSpec((tq, D_ATT), lambda b, i: (b * nq + i, 0)),
        out_shape=jax.ShapeDtypeStruct((nb * T, D_ATT), BF16),
        scratch_shapes=[pltpu.VMEM((tq, T), I32)],
        compiler_params=_cparams("parallel", "arbitrary"),
        name="dsa_prompt",
    )(h, h, h, ikw, h, h, ikw)


def _mem_attn_kernel(q_ref, z_ref, k_ref, v_ref, o_ref):
    for hh in range(MEM_HEADS):
        cs = slice(hh * MEM_HEAD_DIM, (hh + 1) * MEM_HEAD_DIM)
        lg = _nt_dot(q_ref[:, cs].astype(BF16), k_ref[:, cs].astype(BF16)) * MEM_SCALE
        m = jnp.max(lg, axis=-1, keepdims=True)
        p = jnp.exp(lg - m)
        l = jnp.sum(p, axis=-1, keepdims=True)
        o = jnp.dot(p.astype(BF16), v_ref[:, cs].astype(BF16), preferred_element_type=F32) / l
        o_ref[:, cs] = (o * _silu(z_ref[:, cs])).astype(BF16)


def mem_attn_prompt(h, mkv, *, nb, tq):
    T = h.shape[0] // nb
    nq = T // tq
    return pl.pallas_call(
        _mem_attn_kernel,
        grid=(nb, nq),
        in_specs=[
            pl.BlockSpec((tq, D_MEM), lambda b, i: (b * nq + i, OFF_MQ // D_MEM)),
            pl.BlockSpec((tq, D_MEM), lambda b, i: (b * nq + i, OFF_ZM // D_MEM)),
            pl.BlockSpec((N_MEM, D_MEM), lambda b, i: (b, 0)),
            pl.BlockSpec((N_MEM, D_MEM), lambda b, i: (b, 1)),
        ],
        out_specs=pl.BlockSpec((tq, D_MEM), lambda b, i: (b * nq + i, 0)),
        out_shape=jax.ShapeDtypeStruct((nb * T, D_MEM), BF16),
        compiler_params=_cparams("parallel", "arbitrary"),
        name="mem_attn_prompt",
    )(h, h, mkv, mkv)


def mem_attn_sample(mq, zm, mk, mv):
    B, R, _ = mq.shape
    qspec = pl.BlockSpec((None, R, D_MEM), lambda b: (b, 0, 0))
    kspec = pl.BlockSpec((None, N_MEM, D_MEM), lambda b: (b, 0, 0))
    return pl.pallas_call(
        _mem_attn_kernel,
        grid=(B,),
        in_specs=[qspec, qspec, kspec, kspec],
        out_specs=qspec,
        out_shape=jax.ShapeDtypeStruct((B, R, D_MEM), BF16),
        compiler_params=_cparams("parallel"),
        name="mem_attn_sample",
    )(mq, zm, mk, mv)


def _merge_kernel(as_ref, aa_ref, am_ref, gs_ref, ga_ref, gm_ref, ws_ref, wa_ref, wm_ref, o_ref):
    ys = jnp.dot(as_ref[...], ws_ref[...], preferred_element_type=F32)
    ya = jnp.dot(aa_ref[...], wa_ref[...], preferred_element_type=F32)
    ym = jnp.dot(am_ref[...], wm_ref[...], preferred_element_type=F32)
    o = _sigmoid(gs_ref[...]) * ys + _sigmoid(ga_ref[...]) * ya + _sigmoid(gm_ref[...]) * ym
    o_ref[...] = o.astype(BF16)


def merge_branches(a_ssm, a_att, a_mem, h, w_ssm, w_att, w_mem, *, tm, tn=1024):
    M = h.shape[0]
    gb = OFF_G // tn
    per = D_MODEL // tn
    return pl.pallas_call(
        _merge_kernel,
        grid=(M // tm, per),
        in_specs=[
            pl.BlockSpec((tm, D_SSM), lambda i, n: (i, 0)),
            pl.BlockSpec((tm, D_ATT), lambda i, n: (i, 0)),
            pl.BlockSpec((tm, D_MEM), lambda i, n: (i, 0)),
            pl.BlockSpec((tm, tn), lambda i, n: (i, gb + n)),
            pl.BlockSpec((tm, tn), lambda i, n: (i, gb + per + n)),
            pl.BlockSpec((tm, tn), lambda i, n: (i, gb + 2 * per + n)),
            pl.BlockSpec((D_SSM, tn), lambda i, n: (0, n)),
            pl.BlockSpec((D_ATT, tn), lambda i, n: (0, n)),
            pl.BlockSpec((D_MEM, tn), lambda i, n: (0, n)),
        ],
        out_specs=pl.BlockSpec((tm, tn), lambda i, n: (i, n)),
        out_shape=jax.ShapeDtypeStruct((M, D_MODEL), BF16),
        compiler_params=_cparams("parallel", "arbitrary"),
        name="merge_branches",
    )(a_ssm, a_att, a_mem, h, h, h, w_ssm, w_att, w_mem)


def _out_kernel(m_ref, w_ref, x_ref, g_ref, o_ref, *, final_norm):
    y = x_ref[...] + jnp.dot(m_ref[...], w_ref[...], preferred_element_type=F32)
    if final_norm:
        ms = jnp.mean(y * y, axis=-1, keepdims=True)
        y = y * lax.rsqrt(ms + RMS_EPS) * g_ref[...]
    o_ref[...] = y


def out_proj(merged, w_o, x, g, *, tm, final_norm):
    M = x.shape[0]
    return pl.pallas_call(
        functools.partial(_out_kernel, final_norm=final_norm),
        grid=(M // tm,),
        in_specs=[
            pl.BlockSpec((tm, D_MODEL), lambda i: (i, 0)),
            pl.BlockSpec((D_MODEL, D_MODEL), lambda i: (0, 0)),
            pl.BlockSpec((tm, D_MODEL), lambda i: (i, 0)),
            pl.BlockSpec((1, D_MODEL), lambda i: (0, 0)),
        ],
        out_specs=pl.BlockSpec((tm, D_MODEL), lambda i: (i, 0)),
        out_shape=jax.ShapeDtypeStruct((M, D_MODEL), F32),
        compiler_params=_cparams("parallel"),
        name="out_proj",
    )(merged, w_o, x, g.reshape(1, D_MODEL))


def _page_scores(iq_ref, wsel_ref, page):
    s = _nt_dot(iq_ref[...], page.astype(BF16))
    return jnp.dot(wsel_ref[...], jnp.maximum(s, 0.0), preferred_element_type=F32,
                   precision=lax.Precision.HIGHEST)


def _idx_sample_kernel(pt_ref, iq_ref, wsel_ref, *rest, pp):
    pages, o_ref = rest[:pp], rest[pp]
    for i in range(pp):
        o_ref[:, i * PAGE_SIZE:(i + 1) * PAGE_SIZE] = _page_scores(iq_ref, wsel_ref, pages[i][...])


def idx_scores_sample(cache_kidx, layer, pt_flat, iqm, wsel, *, n_pages, pp):
    B = iqm.shape[0]

    def page_spec(i):
        return pl.BlockSpec((None, None, PAGE_SIZE, IDX_DIM),
                            lambda b, s, pt: (layer, pt[b * n_pages + s * pp + i], 0, 0))

    grid_spec = pltpu.PrefetchScalarGridSpec(
        num_scalar_prefetch=1,
        grid=(B, n_pages // pp),
        in_specs=[
            pl.BlockSpec((None,) + iqm.shape[1:], lambda b, s, pt: (b, 0, 0)),
            pl.BlockSpec((None,) + wsel.shape[1:], lambda b, s, pt: (b, 0, 0)),
        ] + [page_spec(i) for i in range(pp)],
        out_specs=pl.BlockSpec((None, SUBLANES, pp * PAGE_SIZE), lambda b, s, pt: (b, 0, s)),
    )
    return pl.pallas_call(
        functools.partial(_idx_sample_kernel, pp=pp),
        grid_spec=grid_spec,
        out_shape=jax.ShapeDtypeStruct((B, SUBLANES, n_pages * PAGE_SIZE), F32),
        compiler_params=_cparams("parallel", "arbitrary"),
        name="idx_scores_sample",
    )(pt_flat, iqm, wsel, *([cache_kidx] * pp))


def _attn_sample_kernel(pt_ref, sc_ref, iq_ref, wsel_ref, iknew_ref, q_ref, knew_ref, vnew_ref, *rest,
                        pp, ds, topk):
    kpages, vpages = rest[:pp], rest[pp:2 * pp]
    o_ref = rest[2 * pp]
    key_ref, keyn_ref, thr_ref, m_ref, l_ref, acc_ref = rest[2 * pp + 1:]
    s = pl.program_id(1)
    ns = pl.num_programs(1)

    @pl.when(s == 0)
    def _():
        sn = _page_scores(iq_ref, wsel_ref, iknew_ref[...])
        row = lax.broadcasted_iota(I32, sn.shape, 0) % ds
        col = lax.broadcasted_iota(I32, sn.shape, 1)
        keyn_ref[...] = _sort_key(jnp.where(col <= row, sn, -jnp.inf))
        key_ref[...] = _sort_key(sc_ref[...])

        def count_ge(cand):
            c = jnp.sum((key_ref[...] >= cand).astype(F32), axis=-1, keepdims=True)
            return c + jnp.sum((keyn_ref[...] >= cand).astype(F32), axis=-1, keepdims=True)

        thr = _kth_largest_key(count_ge, (SUBLANES, 1), float(topk))
        thr_ref[...] = jnp.broadcast_to(thr, thr_ref.shape)
        m_ref[...] = jnp.full(m_ref.shape, NEG_BIG, F32)
        l_ref[...] = jnp.zeros(l_ref.shape, F32)
        acc_ref[...] = jnp.zeros(acc_ref.shape, F32)

    qrows = N_KV_HEADS * SUBLANES
    prows = PAGE_SIZE * N_KV_HEADS
    er = lax.broadcasted_iota(I32, (PAGE_SIZE, prows), 0)
    ec = lax.broadcasted_iota(I32, (PAGE_SIZE, prows), 1)
    expand = (ec // N_KV_HEADS == er).astype(BF16)
    rkv = lax.broadcasted_iota(I32, (qrows, prows), 0) // SUBLANES
    ckv = lax.broadcasted_iota(I32, (qrows, prows), 1) % N_KV_HEADS
    same_head = rkv == ckv
    thr = thr_ref[...]

    def masked_logits(kpage, key_tile):
        sel = (key_tile >= thr).astype(BF16)
        selx = jnp.dot(sel, expand, preferred_element_type=F32)
        selx = jnp.concatenate([selx] * N_KV_HEADS, axis=0)
        lg = _nt_dot(q_ref[...], kpage.astype(BF16))
        return jnp.where((selx > 0.5) & same_head, lg, NEG_BIG)

    def update(lgs, vs):
        lg = lgs[0] if len(lgs) == 1 else jnp.concatenate(lgs, axis=1)
        m_old = m_ref[...]
        m_new = jnp.maximum(m_old, jnp.max(lg, axis=-1, keepdims=True))
        alpha = jnp.exp2((m_old - m_new) * (ATT_SCALE * LOG2E))
        p = jnp.exp2((lg - m_new[:, 0:1]) * (ATT_SCALE * LOG2E))
        l_ref[...] = alpha * l_ref[...] + jnp.sum(p, axis=-1, keepdims=True)
        pb = p.astype(BF16)
        pv = jnp.dot(pb[:, 0:prows], vs[0].astype(BF16), preferred_element_type=F32)
        for i in range(1, len(vs)):
            pv = pv + jnp.dot(pb[:, i * prows:(i + 1) * prows], vs[i].astype(BF16),
                              preferred_element_type=F32)
        acc_ref[...] = alpha * acc_ref[...] + pv
        m_ref[...] = m_new

    base = pl.multiple_of(s * (pp * PAGE_SIZE), pp * PAGE_SIZE)
    update([masked_logits(kpages[i][...], key_ref[:, pl.ds(base + i * PAGE_SIZE, PAGE_SIZE)])
            for i in range(pp)],
           [vpages[i][...] for i in range(pp)])

    @pl.when(s == ns - 1)
    def _():
        update([masked_logits(knew_ref[...], keyn_ref[...])], [vnew_ref[...]])
        o_ref[...] = acc_ref[...] / l_ref[...]


def attn_sample(cache_k, cache_v, layer, pt_flat, scores, iqm, wsel, iknew, qm, knew, vnew,
                *, n_pages, pp, ds):
    B = qm.shape[0]
    L = n_pages * PAGE_SIZE
    topk = min(TOPK_MAX, (L + ds) // 4)

    prows = PAGE_SIZE * N_KV_HEADS
    qrows = N_KV_HEADS * SUBLANES

    def page_spec(i):
        return pl.BlockSpec((None, None, prows, HEAD_DIM),
                            lambda b, s, pt: (layer, pt[b * n_pages + s * pp + i], 0, 0))

    def bspec(a):
        nd = a.ndim - 1
        return pl.BlockSpec((None,) + a.shape[1:], lambda b, s, pt: (b,) + (0,) * nd)

    grid_spec = pltpu.PrefetchScalarGridSpec(
        num_scalar_prefetch=1,
        grid=(B, n_pages // pp),
        in_specs=[bspec(scores), bspec(iqm), bspec(wsel), bspec(iknew), bspec(qm), bspec(knew), bspec(vnew)]
        + [page_spec(i) for i in range(pp)] * 2,
        out_specs=pl.BlockSpec((None, qrows, HEAD_DIM), lambda b, s, pt: (b, 0, 0)),
        scratch_shapes=[
            pltpu.VMEM((SUBLANES, L), I32),
            pltpu.VMEM((SUBLANES, LANES), I32),
            pltpu.VMEM((SUBLANES, LANES), I32),
            pltpu.VMEM((qrows, LANES), F32),
            pltpu.VMEM((qrows, LANES), F32),
            pltpu.VMEM((qrows, HEAD_DIM), F32),
        ],
    )
    return pl.pallas_call(
        functools.partial(_attn_sample_kernel, pp=pp, ds=ds, topk=topk),
        grid_spec=grid_spec,
        out_shape=jax.ShapeDtypeStruct((B, qrows, HEAD_DIM), F32),
        compiler_params=_cparams("parallel", "arbitrary"),
        name="attn_sample",
    )(pt_flat, scores, iqm, wsel, iknew, qm, knew, vnew, *([cache_k] * pp), *([cache_v] * pp))


def _ssm_disc_kernel(ar_ref, ai_ref, ldt_ref, abr_ref, abi_ref, cr_ref, ci_ref):
    ar, ai = ar_ref[...], ai_ref[...]
    dt = jnp.exp(ldt_ref[...])
    mag = jnp.exp(dt * ar)
    abr = mag * jnp.cos(dt * ai)
    abi = mag * jnp.sin(dt * ai)
    den = ar * ar + ai * ai
    nr, ni = abr - 1.0, abi
    abr_ref[...] = abr
    abi_ref[...] = abi
    cr_ref[...] = (nr * ar + ni * ai) / den
    ci_ref[...] = (ni * ar - nr * ai) / den


def ssm_discretise(a_re, a_im, log_dt):
    G, P = a_re.shape
    ldt = jnp.broadcast_to(log_dt[:, None], (G, P))
    shp = jax.ShapeDtypeStruct((G, P), F32)
    return pl.pallas_call(_ssm_disc_kernel, out_shape=[shp] * 4, name="ssm_discretise")(a_re, a_im, ldt)


def _bbar_kernel(cr_ref, ci_ref, br_ref, bi_ref, or_ref, oi_ref):
    cr, ci, br, bi = cr_ref[...], ci_ref[...], br_ref[...], bi_ref[...]
    or_ref[...] = cr * br - ci * bi
    oi_ref[...] = cr * bi + ci * br


def ssm_bbar(coef_r, coef_i, b_re, b_im):
    G, P, C = b_re.shape
    cr = jnp.broadcast_to(coef_r[:, :, None], (G, P, C)).reshape(G, P * C)
    ci = jnp.broadcast_to(coef_i[:, :, None], (G, P, C)).reshape(G, P * C)
    shp = jax.ShapeDtypeStruct((G, P * C), F32)
    r, i = pl.pallas_call(_bbar_kernel, out_shape=[shp] * 2, name="ssm_bbar")(
        cr, ci, b_re.reshape(G, P * C), b_im.reshape(G, P * C))
    return r.reshape(G, P, C), i.reshape(G, P, C)


def _pack_w_in(w):
    o = [0, 1024, 2048, 3072, 3584, 4096, 5120, 6144, 6208, 6224, 6736, 7248, 13392]
    u, zs, q, k, v, za, iq, ik, iw, mq, zm, gates = [w[:, o[i]:o[i + 1]] for i in range(12)]
    main = jnp.concatenate([u, zs, q, za, iq, gates, k, v, mq, zm], axis=1).astype(BF16)
    pad = jnp.zeros((w.shape[0], LANES - IDX_DIM - IDX_HEADS), w.dtype)
    ikw = jnp.concatenate([ik, iw, pad], axis=1).astype(BF16)
    return main, ikw


def _rope_tables(pos):
    def tabs(half):
        inv = ROPE_THETA ** (-jnp.arange(half, dtype=F32) / half)
        ang = pos[:, None] * inv[None, :]
        return jnp.cos(ang), jnp.sin(ang)

    c, s = tabs(64)
    c128 = jnp.concatenate([c, c], axis=1)
    s128 = jnp.concatenate([-s, s], axis=1)
    c, s = tabs(32)
    c64 = jnp.concatenate([c, c, c, c], axis=1)
    s64 = jnp.concatenate([-s, s, -s, s], axis=1)
    lane = jnp.arange(LANES)[None, :]
    c64k = jnp.where(lane < IDX_DIM, c64, 1.0)
    s64k = jnp.where(lane < IDX_DIM, s64, 0.0)
    return (c128, s128, c64, s64), (c128, s128, c64k, s64k)


def _ssm_weights(abr, abi, bbar_r, bbar_i, c_re, c_im):
    a_t = jnp.concatenate([abr.reshape(N_RTILE, 1, LANES), abi.reshape(N_RTILE, 1, LANES)], axis=0)
    a_t = jnp.broadcast_to(a_t, (N_STILE, SUBLANES, LANES))
    eye = jnp.eye(8, dtype=F32)

    def bmat(bb):
        bb = bb.reshape(8, 8, SSM_STATE, SSM_GROUP)
        return jnp.einsum('ngpc,hg->nhcgp', bb, eye).reshape(8, LANES, 512)

    wb = jnp.concatenate([bmat(bbar_r), bmat(bbar_i)], axis=2).astype(BF16)

    def cmat(cc):
        cc = cc.reshape(8, 8, SSM_GROUP, SSM_STATE)
        return jnp.einsum('ngcp,hg->nhpgc', cc, eye).reshape(8, 512, LANES)

    wc = jnp.concatenate([cmat(c_re), -cmat(c_im)], axis=1).astype(BF16)
    return a_t, wb, wc


def _state_to_tiles(s_re, s_im):
    nb = s_re.shape[0]

    def t(s):
        s = s.reshape(nb, N_RTILE, LANES).transpose(1, 0, 2)
        return jnp.pad(s, ((0, 0), (0, SUBLANES - nb), (0, 0)))

    return jnp.concatenate([t(s_re), t(s_im)], axis=0)


def _tiles_to_state(st, nb):
    def t(s):
        return s[:, :nb].transpose(1, 0, 2).reshape(nb, SSM_GROUPS, SSM_STATE)

    return t(st[:N_RTILE]), t(st[N_RTILE:])


def _gate_kernel(y_ref, z_ref, o_ref):
    o_ref[...] = (y_ref[...] * _silu(z_ref[...])).astype(BF16)


def silu_gate(y, z):
    return pl.pallas_call(_gate_kernel, out_shape=jax.ShapeDtypeStruct(y.shape, BF16), name="silu_gate")(y, z)


def _project(x, g, w_main, w_ikw, tabs_main, tabs_ikw, tm):
    h = norm_matmul(x, g, w_main, tabs_main, tm=tm, tn=1024,
                    tiles128=((OFF_Q // 1024, 1024), (OFF_K // 1024, D_KV)), tiles64=((OFF_IQ // 1024, 1024),))
    ikw = norm_matmul(x, g, w_ikw, tabs_ikw, tm=tm, tn=LANES, tiles64=((0, LANES),))
    return h, ikw


def kernel(x_prompt, x_sample, cache_k, cache_v, cache_kidx, cache_mem_k, cache_mem_v, state_ssm_re, state_ssm_im, page_table, mem_prompt, rms_g, w_in, ssm_a_re, ssm_a_im, ssm_log_dt, ssm_b_re, ssm_b_im, ssm_c_re, ssm_c_im, ssm_d, w_glu, w_out_ssm, w_out_att, w_out_mem, mem_norm, w_mem_kv, w_o, final_norm):
    Bp, T, _ = x_prompt.shape
    DB, DS, _ = x_sample.shape
    depth = w_in.shape[0]
    n_pages = page_table.shape[1]
    past = n_pages * PAGE_SIZE
    pool = cache_k.shape[1]
    pp = min(16, n_pages)

    xp = x_prompt.reshape(Bp * T, D_MODEL)
    xs = x_sample.reshape(DB * DS, D_MODEL)
    tabs_p, tabs_pk = _rope_tables(jnp.arange(T, dtype=F32))
    tabs_s, tabs_sk = _rope_tables(jnp.tile(jnp.float32(past) + jnp.arange(DS, dtype=F32), DB))
    mem_rows = mem_prompt.reshape(Bp * N_MEM, D_MODEL)
    one_tab = jnp.ones((Bp * N_MEM, LANES), F32)
    pt_flat = page_table.reshape(-1).astype(I32)
    cache_kr = cache_k.reshape(depth, pool, PAGE_SIZE * N_KV_HEADS, HEAD_DIM)
    cache_vr = cache_v.reshape(depth, pool, PAGE_SIZE * N_KV_HEADS, HEAD_DIM)
    zero_state = jnp.zeros((Bp, SSM_GROUPS, SSM_STATE), F32)

    kp_l, vp_l, ikp_l, mkp_l, mvp_l, srp_l, sip_l = [], [], [], [], [], [], []
    ks_l, vs_l, iks_l, srs_l, sis_l = [], [], [], [], []
    for l in range(depth):
        w_main, w_ikw = _pack_w_in(w_in[l])
        abr, abi, coef_r, coef_i = ssm_discretise(ssm_a_re[l], ssm_a_im[l], ssm_log_dt[l])
        bbar_r, bbar_i = ssm_bbar(coef_r, coef_i, ssm_b_re[l], ssm_b_im[l])
        a_t, wb, wc = _ssm_weights(abr, abi, bbar_r, bbar_i, ssm_c_re[l], ssm_c_im[l])
        wglu = w_glu[l].astype(BF16)
        w_ssm, w_att, w_mem = (w_out_ssm[l].astype(BF16), w_out_att[l].astype(BF16),
                               w_out_mem[l].astype(BF16))
        wo = w_o[l].astype(BF16)
        last = l == depth - 1

        h, ikw = _project(xp, rms_g[l], w_main, w_ikw, tabs_p, tabs_pk, tm=1024)
        a_ssm, st = ssm_branch(h, _state_to_tiles(zero_state, zero_state), a_t, wb, wc, ssm_d[l], wglu,
                               nb=Bp, tc=64)
        a_att = dsa_prompt(h, ikw, nb=Bp, tq=128)
        mkv = norm_matmul(mem_rows, mem_norm[l], w_mem_kv[l].astype(BF16), (one_tab,) * 4,
                          tm=Bp * N_MEM, tn=512)
        a_mem = mem_attn_prompt(h, mkv, nb=Bp, tq=512)
        merged = merge_branches(a_ssm, a_att, a_mem, h, w_ssm, w_att, w_mem, tm=512)
        xp = out_proj(merged, wo, xp, final_norm, tm=512, final_norm=last)
        sr, si = _tiles_to_state(st, Bp)
        kp_l.append(h[:, OFF_K:OFF_K + D_KV].reshape(Bp, T, N_KV_HEADS, HEAD_DIM))
        vp_l.append(h[:, OFF_V:OFF_V + D_KV].reshape(Bp, T, N_KV_HEADS, HEAD_DIM))
        ikp_l.append(ikw[:, :IDX_DIM].reshape(Bp, T, IDX_DIM))
        mkp_l.append(mkv[:, :D_MEM].reshape(Bp, N_MEM, MEM_HEADS, MEM_HEAD_DIM))
        mvp_l.append(mkv[:, D_MEM:].reshape(Bp, N_MEM, MEM_HEADS, MEM_HEAD_DIM))
        srp_l.append(sr)
        sip_l.append(si)

        h, ikw = _project(xs, rms_g[l], w_main, w_ikw, tabs_s, tabs_sk, tm=DS * DB)
        a_ssm, st = ssm_branch(h, _state_to_tiles(state_ssm_re[l], state_ssm_im[l]), a_t, wb, wc,
                               ssm_d[l], wglu, nb=DB, tc=DS)
        hb = h.reshape(DB, DS, N_MAIN)
        ikwb = ikw.reshape(DB, DS, LANES)
        iqm = hb[:, :, OFF_IQ:OFF_IQ + IDX_HEADS * IDX_DIM].reshape(DB, DS * IDX_HEADS, IDX_DIM).astype(BF16)
        iw = ikwb[:, :, IDX_DIM:IDX_DIM + IDX_HEADS] * IW_SCALE
        wsel = jnp.einsum('bth,ts->btsh', iw, jnp.eye(DS, dtype=F32)).reshape(DB, DS, DS * IDX_HEADS)
        wsel = jnp.concatenate([wsel] * (SUBLANES // DS), axis=1)
        iknew = jnp.pad(ikwb[:, :, :IDX_DIM], ((0, 0), (0, PAGE_SIZE - DS), (0, 0)))
        kv_pad = ((0, 0), (0, (PAGE_SIZE - DS) * N_KV_HEADS), (0, 0))
        knew = jnp.pad(hb[:, :, OFF_K:OFF_K + D_KV].reshape(DB, DS * N_KV_HEADS, HEAD_DIM), kv_pad)
        vnew = jnp.pad(hb[:, :, OFF_V:OFF_V + D_KV].reshape(DB, DS * N_KV_HEADS, HEAD_DIM), kv_pad)
        qm = hb[:, :, OFF_Q:OFF_Q + D_ATT].reshape(DB, DS, N_KV_HEADS, GQA, HEAD_DIM)
        qm = qm.transpose(0, 2, 3, 1, 4).reshape(DB, N_KV_HEADS * GQA * DS, HEAD_DIM).astype(BF16)
        scores = idx_scores_sample(cache_kidx, l, pt_flat, iqm, wsel, n_pages=n_pages, pp=pp)
        o = attn_sample(cache_kr, cache_vr, l, pt_flat, scores, iqm, wsel, iknew, qm, knew, vnew,
                        n_pages=n_pages, pp=pp, ds=DS)
        y_att = o.reshape(DB, N_KV_HEADS, GQA, DS, HEAD_DIM).transpose(0, 3, 1, 2, 4).reshape(DB * DS, D_ATT)
        a_att = silu_gate(y_att, h[:, OFF_ZA:OFF_ZA + D_ATT])
        tpad = ((0, 0), (0, SUBLANES - DS), (0, 0))
        a_mem = mem_attn_sample(jnp.pad(hb[:, :, OFF_MQ:OFF_MQ + D_MEM], tpad),
                                jnp.pad(hb[:, :, OFF_ZM:OFF_ZM + D_MEM], tpad),
                                cache_mem_k[l].reshape(DB, N_MEM, D_MEM),
                                cache_mem_v[l].reshape(DB, N_MEM, D_MEM))
        a_mem = a_mem[:, :DS].reshape(DB * DS, D_MEM)
        merged = merge_branches(a_ssm, a_att, a_mem, h, w_ssm, w_att, w_mem, tm=DS * DB)
        xs = out_proj(merged, wo, xs, final_norm, tm=DS * DB, final_norm=last)
        sr, si = _tiles_to_state(st, DB)
        ks_l.append(hb[:, :, OFF_K:OFF_K + D_KV].reshape(DB, DS, N_KV_HEADS, HEAD_DIM))
        vs_l.append(hb[:, :, OFF_V:OFF_V + D_KV].reshape(DB, DS, N_KV_HEADS, HEAD_DIM))
        iks_l.append(ikwb[:, :, :IDX_DIM])
        srs_l.append(sr)
        sis_l.append(si)

    y_prompt = xp.reshape(Bp, T, D_MODEL)
    y_sample = xs.reshape(DB, DS, D_MODEL)
    return (y_prompt, y_sample,
            jnp.stack(kp_l), jnp.stack(vp_l), jnp.stack(ikp_l),
            jnp.stack(mkp_l), jnp.stack(mvp_l), jnp.stack(srp_l), jnp.stack(sip_l),
            jnp.stack(ks_l), jnp.stack(vs_l), jnp.stack(iks_l), jnp.stack(srs_l), jnp.stack(sis_l))
```

```python
import functools
import math

import jax
import jax.numpy as jnp
from jax import lax
from jax.experimental import pallas as pl
from jax.experimental.pallas import tpu as pltpu

F32 = jnp.float32
BF16 = jnp.bfloat16
I32 = jnp.int32

D_MODEL = 2048
D_SSM = 1024
SSM_GROUP = 16
SSM_GROUPS = 64
SSM_STATE = 64
N_HEADS = 8
HEAD_DIM = 128
D_ATT = 1024
N_KV_HEADS = 4
GQA = 2
D_KV = 512
IDX_HEADS = 16
IDX_DIM = 64
TOPK_MAX = 256
N_MEM = 256
MEM_HEADS = 4
MEM_HEAD_DIM = 128
D_MEM = 512
PAGE_SIZE = 128
ROPE_THETA = 10000.0
RMS_EPS = 1e-6
IW_SCALE = (IDX_HEADS * IDX_DIM) ** -0.5
ATT_SCALE = HEAD_DIM ** -0.5
MEM_SCALE = MEM_HEAD_DIM ** -0.5
LOG2E = 1.0 / math.log(2.0)
NEG_BIG = -1e30
INT_MIN = -2 ** 31

LANES = 128
SUBLANES = 8
PACKED_ROWS = 16
VMEM_LIMIT = 56 * 1024 * 1024

OFF_U, OFF_ZS, OFF_Q, OFF_ZA, OFF_IQ, OFF_G = 0, 1024, 2048, 3072, 4096, 5120
OFF_K, OFF_V, OFF_MQ, OFF_ZM = 11264, 11776, 12288, 12800
N_MAIN = 13312
N_STATE = 2 * SSM_GROUPS * SSM_STATE
N_STILE = N_STATE // LANES
N_RTILE = N_STILE // 2


def _cparams(*sem):
    return pltpu.CompilerParams(dimension_semantics=sem, vmem_limit_bytes=VMEM_LIMIT)


def _sigmoid(x):
    return 0.5 * jnp.tanh(0.5 * x) + 0.5


def _silu(x):
    return x * _sigmoid(x)


def _nt_dot(a, b):
    return lax.dot_general(a, b, (((1,), (1,)), ((), ())), preferred_element_type=F32)


def _rope128(acc, c, s):
    outs = []
    for k in range(acc.shape[1] // LANES):
        x = acc[:, k * LANES:(k + 1) * LANES]
        outs.append(x * c + pltpu.roll(x, 64, 1) * s)
    return outs[0] if len(outs) == 1 else jnp.concatenate(outs, axis=1)


def _rope64(acc, c, s):
    lane = lax.broadcasted_iota(I32, (1, LANES), 1)
    first = (lane % 64) < 32
    outs = []
    for k in range(acc.shape[1] // LANES):
        x = acc[:, k * LANES:(k + 1) * LANES]
        rot = jnp.where(first, pltpu.roll(x, 96, 1), pltpu.roll(x, 32, 1))
        outs.append(x * c + rot * s)
    return outs[0] if len(outs) == 1 else jnp.concatenate(outs, axis=1)


def _norm_matmul_kernel(x_ref, g_ref, w_ref, ca_ref, sa_ref, cb_ref, sb_ref, o_ref, xn_ref,
                        *, tiles128, tiles64):
    j = pl.program_id(1)

    @pl.when(j == 0)
    def _():
        x = x_ref[...]
        ms = jnp.mean(x * x, axis=-1, keepdims=True)
        xn_ref[...] = (x * lax.rsqrt(ms + RMS_EPS) * g_ref[...]).astype(BF16)

    def matmul():
        return jnp.dot(xn_ref[...], w_ref[...], preferred_element_type=F32)

    plain = None
    for tile, ncols in tiles128 + tiles64:
        hit = j == tile
        plain = jnp.logical_not(hit) if plain is None else plain & jnp.logical_not(hit)

        @pl.when(hit)
        def _(tile=tile, ncols=ncols):
            acc = matmul()
            if (tile, ncols) in tiles128:
                o_ref[:, :ncols] = _rope128(acc[:, :ncols], ca_ref[...], sa_ref[...])
            else:
                o_ref[:, :ncols] = _rope64(acc[:, :ncols], cb_ref[...], sb_ref[...])
            if ncols < acc.shape[1]:
                o_ref[:, ncols:] = acc[:, ncols:]

    if plain is None:
        o_ref[...] = matmul()
    else:
        @pl.when(plain)
        def _():
            o_ref[...] = matmul()


def norm_matmul(x, g, w, tabs, *, tm, tn, tiles128=(), tiles64=()):
    M, K = x.shape
    N = w.shape[1]
    ca, sa, cb, sb = tabs
    nt = ca.shape[0] // tm
    tab_spec = pl.BlockSpec((tm, LANES), lambda i, j: (i % nt, 0))
    return pl.pallas_call(
        functools.partial(_norm_matmul_kernel, tiles128=tuple(tiles128), tiles64=tuple(tiles64)),
        grid=(M // tm, N // tn),
        in_specs=[
            pl.BlockSpec((tm, K), lambda i, j: (i, 0)),
            pl.BlockSpec((1, K), lambda i, j: (0, 0)),
            pl.BlockSpec((K, tn), lambda i, j: (0, j)),
            tab_spec, tab_spec, tab_spec, tab_spec,
        ],
        out_specs=pl.BlockSpec((tm, tn), lambda i, j: (i, j)),
        out_shape=jax.ShapeDtypeStruct((M, N), F32),
        scratch_shapes=[pltpu.VMEM((tm, K), BF16)],
        compiler_params=_cparams("parallel", "arbitrary"),
        name="norm_matmul",
    )(x, g.reshape(1, K), w, ca, sa, cb, sb)


def _ssm_kernel(u_ref, zs_ref, s0_ref, a_ref, wb_ref, wc_ref, d_ref, wglu_ref, o_ref, sn_ref,
                bh_ref, st_ref, *, nb, tc, lt):
    j = pl.program_id(0)
    rows = nb * tc

    @pl.when(j == 0)
    def _():
        st_ref[...] = s0_ref[...]

    u = u_ref[...].reshape(rows, D_SSM)
    ri = lax.broadcasted_iota(I32, (rows, 1), 0)
    ci = lax.broadcasted_iota(I32, (1, rows), 1)
    to_tb = (ci == (ri % nb) * tc + ri // nb).astype(BF16)
    to_bt = (ci == (ri % tc) * nb + ri // tc).astype(BF16)
    ub = jnp.dot(to_tb, u.astype(BF16), preferred_element_type=F32).astype(BF16)
    for n in range(8):
        res = jnp.dot(ub[:, n * LANES:(n + 1) * LANES], wb_ref[n], preferred_element_type=F32)
        for k in range(4):
            bh_ref[n * 4 + k] = res[:, k * LANES:(k + 1) * LANES]
            bh_ref[N_RTILE + n * 4 + k] = res[:, 512 + k * LANES:512 + (k + 1) * LANES]

    sub = lax.broadcasted_iota(I32, (SUBLANES, LANES), 0)
    low = sub < 4
    for c in range(N_RTILE // lt):
        tr = [c * lt + k for k in range(lt)]
        ti = [N_RTILE + c * lt + k for k in range(lt)]
        ar = [a_ref[t] for t in tr]
        ai = [a_ref[t] for t in ti]
        h0 = tuple(st_ref[t] for t in tr), tuple(st_ref[t] for t in ti)

        def cstep(hr, hi, k, br, bi):
            return ar[k] * hr - ai[k] * hi + br, ar[k] * hi + ai[k] * hr + bi

        if nb == SUBLANES:
            def step(t, carry):
                hr, hi = carry
                off = pl.multiple_of(t * SUBLANES, SUBLANES)
                nr, ni = [], []
                for k in range(lt):
                    r, i = cstep(hr[k], hi[k], k, bh_ref[tr[k], pl.ds(off, SUBLANES), :],
                                 bh_ref[ti[k], pl.ds(off, SUBLANES), :])
                    bh_ref[tr[k], pl.ds(off, SUBLANES), :] = r
                    bh_ref[ti[k], pl.ds(off, SUBLANES), :] = i
                    nr.append(r)
                    ni.append(i)
                return tuple(nr), tuple(ni)

            hN = lax.fori_loop(0, tc, step, h0)
        else:
            def step(t, carry):
                hr, hi = carry
                off = pl.multiple_of(t * SUBLANES, SUBLANES)
                nr, ni = [], []
                for k in range(lt):
                    br = bh_ref[tr[k], pl.ds(off, SUBLANES), :]
                    bi = bh_ref[ti[k], pl.ds(off, SUBLANES), :]
                    r1, i1 = cstep(hr[k], hi[k], k, br, bi)
                    r2, i2 = cstep(pltpu.roll(r1, 4, 0), pltpu.roll(i1, 4, 0), k, br, bi)
                    bh_ref[tr[k], pl.ds(off, SUBLANES), :] = jnp.where(low, r1, r2)
                    bh_ref[ti[k], pl.ds(off, SUBLANES), :] = jnp.where(low, i1, i2)
                    nr.append(pltpu.roll(r2, 4, 0))
                    ni.append(pltpu.roll(i2, 4, 0))
                return tuple(nr), tuple(ni)

            hN = lax.fori_loop(0, rows // SUBLANES, step, h0, unroll=2)
        for k in range(lt):
            st_ref[tr[k]] = hN[0][k]
            st_ref[ti[k]] = hN[1][k]

    sn_ref[...] = st_ref[...]

    ys = []
    for n in range(8):
        hcat = jnp.concatenate([bh_ref[n * 4 + k] for k in range(4)]
                               + [bh_ref[N_RTILE + n * 4 + k] for k in range(4)], axis=1)
        ys.append(jnp.dot(hcat.astype(BF16), wc_ref[n], preferred_element_type=F32))
    yc = jnp.concatenate(ys, axis=1)
    y_hi = yc.astype(BF16)
    rem = yc - y_hi.astype(F32)
    y_mid = rem.astype(BF16)
    y_lo = (rem - y_mid.astype(F32)).astype(BF16)
    yc = (jnp.dot(to_bt, y_hi, preferred_element_type=F32)
          + jnp.dot(to_bt, y_mid, preferred_element_type=F32)
          + jnp.dot(to_bt, y_lo, preferred_element_type=F32))
    y = yc + d_ref[...] * u
    y = jax.nn.gelu(y, approximate=True)
    proj = jnp.dot(y.astype(BF16), wglu_ref[...], preferred_element_type=F32)
    y = proj[:, :D_SSM] * _sigmoid(proj[:, D_SSM:])
    out = (y * _silu(zs_ref[...].reshape(rows, D_SSM))).astype(BF16)
    o_ref[...] = out.reshape(o_ref.shape)


def ssm_branch(h, s0, a_t, wb, wc, d, wglu, *, nb, tc):
    M = h.shape[0]
    T = M // nb
    rows = nb * tc
    const3 = lambda j: (0, 0, 0)
    if T == tc:
        hv = h
        u_spec = pl.BlockSpec((rows, D_SSM), lambda j: (0, OFF_U // D_SSM))
        z_spec = pl.BlockSpec((rows, D_SSM), lambda j: (0, OFF_ZS // D_SSM))
        o_spec = pl.BlockSpec((rows, D_SSM), lambda j: (0, 0))
        o_shape = jax.ShapeDtypeStruct((M, D_SSM), BF16)
    else:
        hv = h.reshape(nb, T, h.shape[1])
        u_spec = pl.BlockSpec((nb, tc, D_SSM), lambda j: (0, j, OFF_U // D_SSM))
        z_spec = pl.BlockSpec((nb, tc, D_SSM), lambda j: (0, j, OFF_ZS // D_SSM))
        o_spec = pl.BlockSpec((nb, tc, D_SSM), lambda j: (0, j, 0))
        o_shape = jax.ShapeDtypeStruct((nb, T, D_SSM), BF16)
    out, st = pl.pallas_call(
        functools.partial(_ssm_kernel, nb=nb, tc=tc, lt=4),
        grid=(T // tc,),
        in_specs=[
            u_spec,
            z_spec,
            pl.BlockSpec((N_STILE, SUBLANES, LANES), const3),
            pl.BlockSpec((N_STILE, SUBLANES, LANES), const3),
            pl.BlockSpec((8, LANES, 1024), const3),
            pl.BlockSpec((8, 1024, LANES), const3),
            pl.BlockSpec((1, D_SSM), lambda j: (0, 0)),
            pl.BlockSpec((D_SSM, 2 * D_SSM), lambda j: (0, 0)),
        ],
        out_specs=[
            o_spec,
            pl.BlockSpec((N_STILE, SUBLANES, LANES), const3),
        ],
        out_shape=[
            o_shape,
            jax.ShapeDtypeStruct((N_STILE, SUBLANES, LANES), F32),
        ],
        scratch_shapes=[
            pltpu.VMEM((N_STILE, rows, LANES), F32),
            pltpu.VMEM((N_STILE, SUBLANES, LANES), F32),
        ],
        compiler_params=_cparams("arbitrary"),
        name="ssm_branch",
    )(hv, hv, s0, a_t, wb, wc, d.reshape(1, D_SSM), wglu)
    return out.reshape(M, D_SSM), st


def _reduce_rows(x, op):
    n, lanes = x.shape
    slab = 8 * SUBLANES
    if n > slab and n % slab == 0:
        x = op(x.reshape(n // slab, slab, lanes), axis=0)
    return op(x, axis=0, keepdims=True)


def _sort_key(score):
    bits = pltpu.bitcast(score, I32)
    return jnp.where(bits < 0, bits ^ jnp.int32(0x7FFFFFFF), bits)


def _kth_largest_key(count_ge, shape, k):
    thr = jnp.full(shape, INT_MIN, I32)
    for bit in range(31, -1, -1):
        inc = INT_MIN if bit == 31 else (1 << bit)
        cand = thr + jnp.int32(inc)
        thr = jnp.where(count_ge(cand) >= k, cand, thr)
    return thr


def _count_rows16(x, cand, strict=False):
    n = x.shape[0]
    m = jnp.where((x > cand) if strict else (x >= cand), jnp.int16(1), jnp.int16(0))
    slab = 8 * PACKED_ROWS
    acc = m[0:slab]
    for i in range(1, n // slab):
        acc = acc + m[i * slab:(i + 1) * slab]
    return jnp.sum(acc.astype(I32), axis=0, keepdims=True)


def _kth_largest16(x, k):
    thr = jnp.full((1, x.shape[1]), -(1 << 15), I32)
    for bit in range(15, -1, -1):
        cand = thr + jnp.int32(1 << bit)
        thr = jnp.where(_count_rows16(x, cand.astype(jnp.int16)) >= k, cand, thr)
    return thr


def _kth_largest_key_cols(key, k):
    hi = (key >> 16).astype(jnp.int16)
    lo = ((key & 0xFFFF) - (1 << 15)).astype(jnp.int16)
    top = _kth_largest16(hi, k)
    top16 = top.astype(jnp.int16)
    k_in_bucket = k - _count_rows16(hi, top16, strict=True)
    lo_in_bucket = jnp.where(hi == top16, lo, jnp.int16(-(1 << 15)))
    low = _kth_largest16(lo_in_bucket, k_in_bucket)
    return top * (1 << 16) + (low + (1 << 15))


def _dsa_prompt_body(q_ref, za_ref, iq_ref, iwq_ref, kb_ref, vt_ref, ikb_ref, o_ref, key_ref, qi,
                     *, tq, topk, nk):
    iq = iq_ref[...].astype(BF16)
    iw_t = jnp.transpose(iwq_ref[...]) * IW_SCALE
    ik = ikb_ref[0:nk, :]
    sc = jnp.zeros((nk, tq), F32)
    for h in range(IDX_HEADS):
        s = _nt_dot(ik, iq[:, h * IDX_DIM:(h + 1) * IDX_DIM])
        sc = sc + jnp.maximum(s, 0.0) * iw_t[IDX_DIM + h:IDX_DIM + h + 1, :]
    kpos = lax.broadcasted_iota(I32, (nk, 1), 0)
    qpos = qi * tq + lax.broadcasted_iota(I32, (1, tq), 1)
    causal = kpos <= qpos
    key_ref[0:nk, :] = _sort_key(jnp.where(causal, sc, -jnp.inf))

    thr = _kth_largest_key_cols(key_ref[0:nk, :], topk)
    sel = (key_ref[0:nk, :] >= thr) & causal

    for kvh in range(N_KV_HEADS):
        hs = slice(kvh * HEAD_DIM, (kvh + 1) * HEAD_DIM)
        kh = kb_ref[0:nk, hs]
        vt = vt_ref[hs, 0:nk]
        for g in range(GQA):
            cs = slice((kvh * GQA + g) * HEAD_DIM, (kvh * GQA + g + 1) * HEAD_DIM)
            lg = jnp.where(sel, _nt_dot(kh, q_ref[:, cs].astype(BF16)), NEG_BIG)
            m = _reduce_rows(lg, jnp.max)
            p = jnp.exp2((lg - m) * (ATT_SCALE * LOG2E))
            l = _reduce_rows(p, jnp.sum)
            o_t = jnp.dot(vt, p.astype(BF16), preferred_element_type=F32) / l
            o_ref[:, cs] = (jnp.transpose(o_t) * _silu(za_ref[:, cs])).astype(BF16)


def _dsa_prompt_kernel(q_ref, za_ref, iq_ref, iwq_ref, k_ref, v_ref, ik_ref, o_ref,
                       key_ref, kb_ref, vt_ref, ikb_ref, *, tq, topk, kc):
    qi = pl.program_id(1)
    T = k_ref.shape[0]

    @pl.when(qi == 0)
    def _():
        kb_ref[...] = k_ref[...].astype(BF16)
        ikb_ref[...] = ik_ref[:, 0:IDX_DIM].astype(BF16)
        for c in range(T // LANES):
            rs = slice(c * LANES, (c + 1) * LANES)
            for kvh in range(N_KV_HEADS):
                hs = slice(kvh * HEAD_DIM, (kvh + 1) * HEAD_DIM)
                vt_ref[hs, rs] = jnp.transpose(v_ref[rs, hs]).astype(BF16)

    need = (qi + 1) * tq
    for v in range(1, T // kc + 1):
        @pl.when((need > (v - 1) * kc) & (need <= v * kc))
        def _(v=v):
            _dsa_prompt_body(q_ref, za_ref, iq_ref, iwq_ref, kb_ref, vt_ref, ikb_ref, o_ref, key_ref, qi,
                             tq=tq, topk=topk, nk=v * kc)


def dsa_prompt(h, ikw, *, nb, tq, kc=512):
    T = h.shape[0] // nb
    topk = min(TOPK_MAX, T // 4)
    nq = T // tq
    return pl.pallas_call(
        functools.partial(_dsa_prompt_kernel, tq=tq, topk=topk, kc=kc),
        grid=(nb, nq),
        in_specs=[
            pl.BlockSpec((tq, D_ATT), lambda b, i: (b * nq + i, OFF_Q // D_ATT)),
            pl.BlockSpec((tq, D_ATT), lambda b, i: (b * nq + i, OFF_ZA // D_ATT)),
            pl.BlockSpec((tq, D_ATT), lambda b, i: (b * nq + i, OFF_IQ // D_ATT)),
            pl.BlockSpec((tq, LANES), lambda b, i: (b * nq + i, 0)),
            pl.BlockSpec((T, D_KV), lambda b, i: (b, OFF_K // D_KV)),
            pl.BlockSpec((T, D_KV), lambda b, i: (b, OFF_V // D_KV)),
            pl.BlockSpec((T, LANES), lambda b, i: (b, 0)),
        ],
        out_specs=pl.BlockSpec((tq, D_ATT), lambda b, i: (b * nq + i, 0)),
        out_shape=jax.ShapeDtypeStruct((nb * T, D_ATT), BF16),
        scratch_shapes=[
            pltpu.VMEM((T, tq), I32),
            pltpu.VMEM((T, D_KV), BF16),
            pltpu.VMEM((D_KV, T), BF16),
            pltpu.VMEM((T, IDX_DIM), BF16),
        ],
        compiler_params=_cparams("parallel", "arbitrary"),
        name="dsa_prompt",
    )(h, h, h, ikw, h, h, ikw)


def _mem_attn_kernel(q_ref, z_ref, k_ref, v_ref, o_ref):
    for hh in range(MEM_HEADS):
        cs = slice(hh * MEM_HEAD_DIM, (hh + 1) * MEM_HEAD_DIM)
        lg = _nt_dot(q_ref[:, cs].astype(BF16), k_ref[:, cs].astype(BF16)) * MEM_SCALE
        m = jnp.max(lg, axis=-1, keepdims=True)
        p = jnp.exp(lg - m)
        l = jnp.sum(p, axis=-1, keepdims=True)
        o = jnp.dot(p.astype(BF16), v_ref[:, cs].astype(BF16), preferred_element_type=F32) / l
        o_ref[:, cs] = (o * _silu(z_ref[:, cs])).astype(BF16)


def mem_attn_prompt(h, mkv, *, nb, tq):
    T = h.shape[0] // nb
    nq = T // tq
    return pl.pallas_call(
        _mem_attn_kernel,
        grid=(nb, nq),
        in_specs=[
            pl.BlockSpec((tq, D_MEM), lambda b, i: (b * nq + i, OFF_MQ // D_MEM)),
            pl.BlockSpec((tq, D_MEM), lambda b, i: (b * nq + i, OFF_ZM // D_MEM)),
            pl.BlockSpec((N_MEM, D_MEM), lambda b, i: (b, 0)),
            pl.BlockSpec((N_MEM, D_MEM), lambda b, i: (b, 1)),
        ],
        out_specs=pl.BlockSpec((tq, D_MEM), lambda b, i: (b * nq + i, 0)),
        out_shape=jax.ShapeDtypeStruct((nb * T, D_MEM), BF16),
        compiler_params=_cparams("parallel", "arbitrary"),
        name="mem_attn_prompt",
    )(h, h, mkv, mkv)


def mem_attn_sample(mq, zm, mk, mv):
    B, R, _ = mq.shape
    qspec = pl.BlockSpec((None, R, D_MEM), lambda b: (b, 0, 0))
    kspec = pl.BlockSpec((None, N_MEM, D_MEM), lambda b: (b, 0, 0))
    return pl.pallas_call(
        _mem_attn_kernel,
        grid=(B,),
        in_specs=[qspec, qspec, kspec, kspec],
        out_specs=qspec,
        out_shape=jax.ShapeDtypeStruct((B, R, D_MEM), BF16),
        compiler_params=_cparams("parallel"),
        name="mem_attn_sample",
    )(mq, zm, mk, mv)


def _merge_kernel(as_ref, aa_ref, am_ref, gs_ref, ga_ref, gm_ref, ws_ref, wa_ref, wm_ref, o_ref):
    ys = jnp.dot(as_ref[...], ws_ref[...], preferred_element_type=F32)
    ya = jnp.dot(aa_ref[...], wa_ref[...], preferred_element_type=F32)
    ym = jnp.dot(am_ref[...], wm_ref[...], preferred_element_type=F32)
    o = _sigmoid(gs_ref[...]) * ys + _sigmoid(ga_ref[...]) * ya + _sigmoid(gm_ref[...]) * ym
    o_ref[...] = o.astype(BF16)


def merge_branches(a_ssm, a_att, a_mem, h, w_ssm, w_att, w_mem, *, tm, tn=1024):
    M = h.shape[0]
    gb = OFF_G // tn
    per = D_MODEL // tn
    return pl.pallas_call(
        _merge_kernel,
        grid=(per, M // tm),
        in_specs=[
            pl.BlockSpec((tm, D_SSM), lambda n, i: (i, 0)),
            pl.BlockSpec((tm, D_ATT), lambda n, i: (i, 0)),
            pl.BlockSpec((tm, D_MEM), lambda n, i: (i, 0)),
            pl.BlockSpec((tm, tn), lambda n, i: (i, gb + n)),
            pl.BlockSpec((tm, tn), lambda n, i: (i, gb + per + n)),
            pl.BlockSpec((tm, tn), lambda n, i: (i, gb + 2 * per + n)),
            pl.BlockSpec((D_SSM, tn), lambda n, i: (0, n)),
            pl.BlockSpec((D_ATT, tn), lambda n, i: (0, n)),
            pl.BlockSpec((D_MEM, tn), lambda n, i: (0, n)),
        ],
        out_specs=pl.BlockSpec((tm, tn), lambda n, i: (i, n)),
        out_shape=jax.ShapeDtypeStruct((M, D_MODEL), BF16),
        compiler_params=_cparams("parallel", "arbitrary"),
        name="merge_branches",
    )(a_ssm, a_att, a_mem, h, h, h, w_ssm, w_att, w_mem)


def _out_kernel(m_ref, w_ref, x_ref, g_ref, o_ref, *, final_norm):
    y = x_ref[...] + jnp.dot(m_ref[...], w_ref[...], preferred_element_type=F32)
    if final_norm:
        ms = jnp.mean(y * y, axis=-1, keepdims=True)
        y = y * lax.rsqrt(ms + RMS_EPS) * g_ref[...]
    o_ref[...] = y


def out_proj(merged, w_o, x, g, *, tm, final_norm):
    M = x.shape[0]
    return pl.pallas_call(
        functools.partial(_out_kernel, final_norm=final_norm),
        grid=(M // tm,),
        in_specs=[
            pl.BlockSpec((tm, D_MODEL), lambda i: (i, 0)),
            pl.BlockSpec((D_MODEL, D_MODEL), lambda i: (0, 0)),
            pl.BlockSpec((tm, D_MODEL), lambda i: (i, 0)),
            pl.BlockSpec((1, D_MODEL), lambda i: (0, 0)),
        ],
        out_specs=pl.BlockSpec((tm, D_MODEL), lambda i: (i, 0)),
        out_shape=jax.ShapeDtypeStruct((M, D_MODEL), F32),
        compiler_params=_cparams("parallel"),
        name="out_proj",
    )(merged, w_o, x, g.reshape(1, D_MODEL))


def _page_scores(iq_ref, iwb_ref, pages_t):
    kt = pages_t[0] if len(pages_t) == 1 else jnp.concatenate(pages_t, axis=1)
    s = jnp.dot(iq_ref[...], kt.astype(BF16), preferred_element_type=F32)
    iwb = iwb_ref[...]
    outs = []
    for i in range(len(pages_t)):
        w = jnp.maximum(s[:, i * PAGE_SIZE:(i + 1) * PAGE_SIZE], 0.0) * iwb
        outs.append(jnp.sum(w.reshape(IDX_HEADS, SUBLANES, PAGE_SIZE), axis=0))
    return outs


def _idx_sample_kernel(pt_ref, iq_ref, iwb_ref, *rest, pp):
    pages, o_ref = rest[:pp], rest[pp]
    outs = _page_scores(iq_ref, iwb_ref, [pages[i][...] for i in range(pp)])
    for i in range(pp):
        o_ref[:, i * PAGE_SIZE:(i + 1) * PAGE_SIZE] = outs[i]


def idx_scores_sample(kidx_t, layer, pt_flat, iqm, iwb, *, n_pages, pp):
    B = iqm.shape[0]

    def page_spec(i):
        return pl.BlockSpec((None, None, IDX_DIM, PAGE_SIZE),
                            lambda b, s, pt: (layer, pt[b * n_pages + s * pp + i], 0, 0))

    grid_spec = pltpu.PrefetchScalarGridSpec(
        num_scalar_prefetch=1,
        grid=(B, n_pages // pp),
        in_specs=[
            pl.BlockSpec((None,) + iqm.shape[1:], lambda b, s, pt: (b, 0, 0)),
            pl.BlockSpec((None,) + iwb.shape[1:], lambda b, s, pt: (b, 0, 0)),
        ] + [page_spec(i) for i in range(pp)],
        out_specs=pl.BlockSpec((None, SUBLANES, pp * PAGE_SIZE), lambda b, s, pt: (b, 0, s)),
    )
    return pl.pallas_call(
        functools.partial(_idx_sample_kernel, pp=pp),
        grid_spec=grid_spec,
        out_shape=jax.ShapeDtypeStruct((B, SUBLANES, n_pages * PAGE_SIZE), F32),
        compiler_params=_cparams("parallel", "arbitrary"),
        name="idx_scores_sample",
    )(pt_flat, iqm, iwb, *([kidx_t] * pp))


def _attn_sample_kernel(pt_ref, sc_ref, iq_ref, iwb_ref, iknew_ref, q_ref, knew_ref, vnew_ref, *rest,
                        pp, ds, topk):
    kpages, vpages = rest[:pp], rest[pp:2 * pp]
    o_ref = rest[2 * pp]
    key_ref, keyn_ref, thr_ref, m_ref, l_ref, acc_ref = rest[2 * pp + 1:]
    s = pl.program_id(1)
    ns = pl.num_programs(1)

    @pl.when(s == 0)
    def _():
        sn = _page_scores(iq_ref, iwb_ref, [iknew_ref[...]])[0]
        row = lax.broadcasted_iota(I32, sn.shape, 0) % ds
        col = lax.broadcasted_iota(I32, sn.shape, 1)
        keyn_ref[...] = _sort_key(jnp.where(col <= row, sn, -jnp.inf))
        key_ref[...] = _sort_key(sc_ref[...])

        def count_ge(cand):
            c = jnp.sum((key_ref[...] >= cand).astype(F32), axis=-1, keepdims=True)
            return c + jnp.sum((keyn_ref[...] >= cand).astype(F32), axis=-1, keepdims=True)

        thr = _kth_largest_key(count_ge, (SUBLANES, 1), float(topk))
        thr_ref[...] = jnp.broadcast_to(thr, thr_ref.shape)
        m_ref[...] = jnp.full(m_ref.shape, NEG_BIG, F32)
        l_ref[...] = jnp.zeros(l_ref.shape, F32)
        acc_ref[...] = jnp.zeros(acc_ref.shape, F32)

    qrows = N_KV_HEADS * SUBLANES
    prows = PAGE_SIZE * N_KV_HEADS
    er = lax.broadcasted_iota(I32, (PAGE_SIZE, prows), 0)
    ec = lax.broadcasted_iota(I32, (PAGE_SIZE, prows), 1)
    expand = (ec // N_KV_HEADS == er).astype(BF16)
    rkv = lax.broadcasted_iota(I32, (qrows, prows), 0) // SUBLANES
    ckv = lax.broadcasted_iota(I32, (qrows, prows), 1) % N_KV_HEADS
    same_head = rkv == ckv
    thr = thr_ref[...]

    def masked_logits(kpage, key_tile):
        sel = (key_tile >= thr).astype(BF16)
        selx = jnp.dot(sel, expand, preferred_element_type=F32)
        selx = jnp.concatenate([selx] * N_KV_HEADS, axis=0)
        lg = _nt_dot(q_ref[...], kpage.astype(BF16))
        return jnp.where((selx > 0.5) & same_head, lg, NEG_BIG)

    def update(lgs, vs):
        lg = lgs[0] if len(lgs) == 1 else jnp.concatenate(lgs, axis=1)
        m_old = m_ref[...]
        m_new = jnp.maximum(m_old, jnp.max(lg, axis=-1, keepdims=True))
        alpha = jnp.exp2((m_old - m_new) * (ATT_SCALE * LOG2E))
        p = jnp.exp2((lg - m_new[:, 0:1]) * (ATT_SCALE * LOG2E))
        l_ref[...] = alpha * l_ref[...] + jnp.sum(p, axis=-1, keepdims=True)
        pb = p.astype(BF16)
        pv = jnp.dot(pb[:, 0:prows], vs[0].astype(BF16), preferred_element_type=F32)
        for i in range(1, len(vs)):
            pv = pv + jnp.dot(pb[:, i * prows:(i + 1) * prows], vs[i].astype(BF16),
                              preferred_element_type=F32)
        acc_ref[...] = alpha * acc_ref[...] + pv
        m_ref[...] = m_new

    base = pl.multiple_of(s * (pp * PAGE_SIZE), pp * PAGE_SIZE)
    update([masked_logits(kpages[i][...], key_ref[:, pl.ds(base + i * PAGE_SIZE, PAGE_SIZE)])
            for i in range(pp)],
           [vpages[i][...] for i in range(pp)])

    @pl.when(s == ns - 1)
    def _():
        update([masked_logits(knew_ref[...], keyn_ref[...])], [vnew_ref[...]])
        o_ref[...] = acc_ref[...] / l_ref[...]


def attn_sample(cache_k, cache_v, layer, pt_flat, scores, iqm, iwb, iknew, qm, knew, vnew,
                *, n_pages, pp, ds):
    B = qm.shape[0]
    L = n_pages * PAGE_SIZE
    topk = min(TOPK_MAX, (L + ds) // 4)

    prows = PAGE_SIZE * N_KV_HEADS
    qrows = N_KV_HEADS * SUBLANES

    def page_spec(i):
        return pl.BlockSpec((None, None, prows, HEAD_DIM),
                            lambda b, s, pt: (layer, pt[b * n_pages + s * pp + i], 0, 0))

    def bspec(a):
        nd = a.ndim - 1
        return pl.BlockSpec((None,) + a.shape[1:], lambda b, s, pt: (b,) + (0,) * nd)

    grid_spec = pltpu.PrefetchScalarGridSpec(
        num_scalar_prefetch=1,
        grid=(B, n_pages // pp),
        in_specs=[bspec(scores), bspec(iqm), bspec(iwb), bspec(iknew), bspec(qm), bspec(knew), bspec(vnew)]
        + [page_spec(i) for i in range(pp)] * 2,
        out_specs=pl.BlockSpec((None, qrows, HEAD_DIM), lambda b, s, pt: (b, 0, 0)),
        scratch_shapes=[
            pltpu.VMEM((SUBLANES, L), I32),
            pltpu.VMEM((SUBLANES, LANES), I32),
            pltpu.VMEM((SUBLANES, LANES), I32),
            pltpu.VMEM((qrows, LANES), F32),
            pltpu.VMEM((qrows, LANES), F32),
            pltpu.VMEM((qrows, HEAD_DIM), F32),
        ],
    )
    return pl.pallas_call(
        functools.partial(_attn_sample_kernel, pp=pp, ds=ds, topk=topk),
        grid_spec=grid_spec,
        out_shape=jax.ShapeDtypeStruct((B, qrows, HEAD_DIM), F32),
        compiler_params=_cparams("parallel", "arbitrary"),
        name="attn_sample",
    )(pt_flat, scores, iqm, iwb, iknew, qm, knew, vnew, *([cache_k] * pp), *([cache_v] * pp))


def _ssm_disc_kernel(ar_ref, ai_ref, ldt_ref, abr_ref, abi_ref, cr_ref, ci_ref):
    ar, ai = ar_ref[...], ai_ref[...]
    dt = jnp.exp(ldt_ref[...])
    mag = jnp.exp(dt * ar)
    abr = mag * jnp.cos(dt * ai)
    abi = mag * jnp.sin(dt * ai)
    den = ar * ar + ai * ai
    nr, ni = abr - 1.0, abi
    abr_ref[...] = abr
    abi_ref[...] = abi
    cr_ref[...] = (nr * ar + ni * ai) / den
    ci_ref[...] = (ni * ar - nr * ai) / den


def ssm_discretise(a_re, a_im, log_dt):
    G, P = a_re.shape
    ldt = jnp.broadcast_to(log_dt[:, None], (G, P))
    shp = jax.ShapeDtypeStruct((G, P), F32)
    return pl.pallas_call(_ssm_disc_kernel, out_shape=[shp] * 4, name="ssm_discretise")(a_re, a_im, ldt)


def _bbar_kernel(cr_ref, ci_ref, br_ref, bi_ref, or_ref, oi_ref):
    cr, ci, br, bi = cr_ref[...], ci_ref[...], br_ref[...], bi_ref[...]
    or_ref[...] = cr * br - ci * bi
    oi_ref[...] = cr * bi + ci * br


def ssm_bbar(coef_r, coef_i, b_re, b_im):
    G, P, C = b_re.shape
    cr = jnp.broadcast_to(coef_r[:, :, None], (G, P, C)).reshape(G, P * C)
    ci = jnp.broadcast_to(coef_i[:, :, None], (G, P, C)).reshape(G, P * C)
    shp = jax.ShapeDtypeStruct((G, P * C), F32)
    r, i = pl.pallas_call(_bbar_kernel, out_shape=[shp] * 2, name="ssm_bbar")(
        cr, ci, b_re.reshape(G, P * C), b_im.reshape(G, P * C))
    return r.reshape(G, P, C), i.reshape(G, P, C)


def _pack_w_in(w):
    o = [0, 1024, 2048, 3072, 3584, 4096, 5120, 6144, 6208, 6224, 6736, 7248, 13392]
    u, zs, q, k, v, za, iq, ik, iw, mq, zm, gates = [w[:, o[i]:o[i + 1]] for i in range(12)]
    main = jnp.concatenate([u, zs, q, za, iq, gates, k, v, mq, zm], axis=1).astype(BF16)
    pad = jnp.zeros((w.shape[0], LANES - IDX_DIM - IDX_HEADS), w.dtype)
    ikw = jnp.concatenate([ik, iw, pad], axis=1).astype(BF16)
    return main, ikw


def _rope_tables(pos):
    def tabs(half):
        inv = ROPE_THETA ** (-jnp.arange(half, dtype=F32) / half)
        ang = pos[:, None] * inv[None, :]
        return jnp.cos(ang), jnp.sin(ang)

    c, s = tabs(64)
    c128 = jnp.concatenate([c, c], axis=1)
    s128 = jnp.concatenate([-s, s], axis=1)
    c, s = tabs(32)
    c64 = jnp.concatenate([c, c, c, c], axis=1)
    s64 = jnp.concatenate([-s, s, -s, s], axis=1)
    lane = jnp.arange(LANES)[None, :]
    c64k = jnp.where(lane < IDX_DIM, c64, 1.0)
    s64k = jnp.where(lane < IDX_DIM, s64, 0.0)
    return (c128, s128, c64, s64), (c128, s128, c64k, s64k)


def _ssm_weights(abr, abi, bbar_r, bbar_i, c_re, c_im):
    a_t = jnp.concatenate([abr.reshape(N_RTILE, 1, LANES), abi.reshape(N_RTILE, 1, LANES)], axis=0)
    a_t = jnp.broadcast_to(a_t, (N_STILE, SUBLANES, LANES))
    eye = jnp.eye(8, dtype=F32)

    def bmat(bb):
        bb = bb.reshape(8, 8, SSM_STATE, SSM_GROUP)
        return jnp.einsum('ngpc,hg->nhcgp', bb, eye).reshape(8, LANES, 512)

    wb = jnp.concatenate([bmat(bbar_r), bmat(bbar_i)], axis=2).astype(BF16)

    def cmat(cc):
        cc = cc.reshape(8, 8, SSM_GROUP, SSM_STATE)
        return jnp.einsum('ngcp,hg->nhpgc', cc, eye).reshape(8, 512, LANES)

    wc = jnp.concatenate([cmat(c_re), -cmat(c_im)], axis=1).astype(BF16)
    return a_t, wb, wc


def _state_to_tiles(s_re, s_im):
    nb = s_re.shape[0]

    def t(s):
        s = s.reshape(nb, N_RTILE, LANES).transpose(1, 0, 2)
        return jnp.pad(s, ((0, 0), (0, SUBLANES - nb), (0, 0)))

    return jnp.concatenate([t(s_re), t(s_im)], axis=0)


def _tiles_to_state(st, nb):
    def t(s):
        return s[:, :nb].transpose(1, 0, 2).reshape(nb, SSM_GROUPS, SSM_STATE)

    return t(st[:N_RTILE]), t(st[N_RTILE:])


def _gate_kernel(y_ref, z_ref, o_ref):
    o_ref[...] = (y_ref[...] * _silu(z_ref[...])).astype(BF16)


def silu_gate(y, z):
    return pl.pallas_call(_gate_kernel, out_shape=jax.ShapeDtypeStruct(y.shape, BF16), name="silu_gate")(y, z)


def _project(x, g, w_main, w_ikw, tabs_main, tabs_ikw, tm):
    h = norm_matmul(x, g, w_main, tabs_main, tm=tm, tn=1024,
                    tiles128=((OFF_Q // 1024, 1024), (OFF_K // 1024, D_KV)), tiles64=((OFF_IQ // 1024, 1024),))
    ikw = norm_matmul(x, g, w_ikw, tabs_ikw, tm=tm, tn=LANES, tiles64=((0, LANES),))
    return h, ikw


def kernel(x_prompt, x_sample, cache_k, cache_v, cache_kidx, cache_mem_k, cache_mem_v, state_ssm_re, state_ssm_im, page_table, mem_prompt, rms_g, w_in, ssm_a_re, ssm_a_im, ssm_log_dt, ssm_b_re, ssm_b_im, ssm_c_re, ssm_c_im, ssm_d, w_glu, w_out_ssm, w_out_att, w_out_mem, mem_norm, w_mem_kv, w_o, final_norm):
    Bp, T, _ = x_prompt.shape
    DB, DS, _ = x_sample.shape
    depth = w_in.shape[0]
    n_pages = page_table.shape[1]
    past = n_pages * PAGE_SIZE
    pool = cache_k.shape[1]
    pp = min(16, n_pages)

    xp = x_prompt.reshape(Bp * T, D_MODEL)
    xs = x_sample.reshape(DB * DS, D_MODEL)
    tabs_p, tabs_pk = _rope_tables(jnp.arange(T, dtype=F32))
    tabs_s, tabs_sk = _rope_tables(jnp.tile(jnp.float32(past) + jnp.arange(DS, dtype=F32), DB))
    mem_rows = mem_prompt.reshape(Bp * N_MEM, D_MODEL)
    one_tab = jnp.ones((Bp * N_MEM, LANES), F32)
    pt_flat = page_table.reshape(-1).astype(I32)
    cache_kr = cache_k.reshape(depth, pool, PAGE_SIZE * N_KV_HEADS, HEAD_DIM)
    cache_vr = cache_v.reshape(depth, pool, PAGE_SIZE * N_KV_HEADS, HEAD_DIM)
    kidx_t = jnp.swapaxes(cache_kidx, 2, 3)
    zero_state = jnp.zeros((Bp, SSM_GROUPS, SSM_STATE), F32)

    kp_l, vp_l, ikp_l, mkp_l, mvp_l, srp_l, sip_l = [], [], [], [], [], [], []
    ks_l, vs_l, iks_l, srs_l, sis_l = [], [], [], [], []
    for l in range(depth):
        w_main, w_ikw = _pack_w_in(w_in[l])
        abr, abi, coef_r, coef_i = ssm_discretise(ssm_a_re[l], ssm_a_im[l], ssm_log_dt[l])
        bbar_r, bbar_i = ssm_bbar(coef_r, coef_i, ssm_b_re[l], ssm_b_im[l])
        a_t, wb, wc = _ssm_weights(abr, abi, bbar_r, bbar_i, ssm_c_re[l], ssm_c_im[l])
        wglu = w_glu[l].astype(BF16)
        w_ssm, w_att, w_mem = (w_out_ssm[l].astype(BF16), w_out_att[l].astype(BF16),
                               w_out_mem[l].astype(BF16))
        wo = w_o[l].astype(BF16)
        last = l == depth - 1

        h, ikw = _project(xp, rms_g[l], w_main, w_ikw, tabs_p, tabs_pk, tm=1024)
        a_ssm, st = ssm_branch(h, _state_to_tiles(zero_state, zero_state), a_t, wb, wc, ssm_d[l], wglu,
                               nb=Bp, tc=64)
        a_att = dsa_prompt(h, ikw, nb=Bp, tq=128)
        mkv = norm_matmul(mem_rows, mem_norm[l], w_mem_kv[l].astype(BF16), (one_tab,) * 4,
                          tm=Bp * N_MEM, tn=512)
        a_mem = mem_attn_prompt(h, mkv, nb=Bp, tq=512)
        merged = merge_branches(a_ssm, a_att, a_mem, h, w_ssm, w_att, w_mem, tm=512)
        xp = out_proj(merged, wo, xp, final_norm, tm=512, final_norm=last)
        sr, si = _tiles_to_state(st, Bp)
        kp_l.append(h[:, OFF_K:OFF_K + D_KV].reshape(Bp, T, N_KV_HEADS, HEAD_DIM))
        vp_l.append(h[:, OFF_V:OFF_V + D_KV].reshape(Bp, T, N_KV_HEADS, HEAD_DIM))
        ikp_l.append(ikw[:, :IDX_DIM].reshape(Bp, T, IDX_DIM))
        mkp_l.append(mkv[:, :D_MEM].reshape(Bp, N_MEM, MEM_HEADS, MEM_HEAD_DIM))
        mvp_l.append(mkv[:, D_MEM:].reshape(Bp, N_MEM, MEM_HEADS, MEM_HEAD_DIM))
        srp_l.append(sr)
        sip_l.append(si)

        h, ikw = _project(xs, rms_g[l], w_main, w_ikw, tabs_s, tabs_sk, tm=DS * DB)
        a_ssm, st = ssm_branch(h, _state_to_tiles(state_ssm_re[l], state_ssm_im[l]), a_t, wb, wc,
                               ssm_d[l], wglu, nb=DB, tc=DS)
        hb = h.reshape(DB, DS, N_MAIN)
        ikwb = ikw.reshape(DB, DS, LANES)
        iqm = hb[:, :, OFF_IQ:OFF_IQ + IDX_HEADS * IDX_DIM].reshape(DB, DS, IDX_HEADS, IDX_DIM)
        iqm = jnp.concatenate([iqm.transpose(0, 2, 1, 3)] * GQA, axis=2)
        iqm = iqm.reshape(DB, IDX_HEADS * SUBLANES, IDX_DIM).astype(BF16)
        iw = ikwb[:, :, IDX_DIM:IDX_DIM + IDX_HEADS] * IW_SCALE
        iwb = jnp.concatenate([iw.transpose(0, 2, 1)] * GQA, axis=2).reshape(DB, IDX_HEADS * SUBLANES, 1)
        iwb = jnp.broadcast_to(iwb, (DB, IDX_HEADS * SUBLANES, LANES))
        iknew = jnp.pad(ikwb[:, :, :IDX_DIM].transpose(0, 2, 1), ((0, 0), (0, 0), (0, PAGE_SIZE - DS)))
        kv_pad = ((0, 0), (0, (PAGE_SIZE - DS) * N_KV_HEADS), (0, 0))
        knew = jnp.pad(hb[:, :, OFF_K:OFF_K + D_KV].reshape(DB, DS * N_KV_HEADS, HEAD_DIM), kv_pad)
        vnew = jnp.pad(hb[:, :, OFF_V:OFF_V + D_KV].reshape(DB, DS * N_KV_HEADS, HEAD_DIM), kv_pad)
        qm = hb[:, :, OFF_Q:OFF_Q + D_ATT].reshape(DB, DS, N_KV_HEADS, GQA, HEAD_DIM)
        qm = qm.transpose(0, 2, 3, 1, 4).reshape(DB, N_KV_HEADS * GQA * DS, HEAD_DIM).astype(BF16)
        scores = idx_scores_sample(kidx_t, l, pt_flat, iqm, iwb, n_pages=n_pages, pp=pp)
        o = attn_sample(cache_kr, cache_vr, l, pt_flat, scores, iqm, iwb, iknew, qm, knew, vnew,
                        n_pages=n_pages, pp=pp, ds=DS)
        y_att = o.reshape(DB, N_KV_HEADS, GQA, DS, HEAD_DIM).transpose(0, 3, 1, 2, 4).reshape(DB * DS, D_ATT)
        a_att = silu_gate(y_att, h[:, OFF_ZA:OFF_ZA + D_ATT])
        tpad = ((0, 0), (0, SUBLANES - DS), (0, 0))
        a_mem = mem_attn_sample(jnp.pad(hb[:, :, OFF_MQ:OFF_MQ + D_MEM], tpad),
                                jnp.pad(hb[:, :, OFF_ZM:OFF_ZM + D_MEM], tpad),
                                cache_mem_k[l].reshape(DB, N_MEM, D_MEM),
                                cache_mem_v[l].reshape(DB, N_MEM, D_MEM))
        a_mem = a_mem[:, :DS].reshape(DB * DS, D_MEM)
        merged = merge_branches(a_ssm, a_att, a_mem, h, w_ssm, w_att, w_mem, tm=DS * DB)
        xs = out_proj(merged, wo, xs, final_norm, tm=DS * DB, final_norm=last)
        sr, si = _tiles_to_state(st, DB)
        ks_l.append(hb[:, :, OFF_K:OFF_K + D_KV].reshape(DB, DS, N_KV_HEADS, HEAD_DIM))
        vs_l.append(hb[:, :, OFF_V:OFF_V + D_KV].reshape(DB, DS, N_KV_HEADS, HEAD_DIM))
        iks_l.append(ikwb[:, :, :IDX_DIM])
        srs_l.append(sr)
        sis_l.append(si)

    y_prompt = xp.reshape(Bp, T, D_MODEL)
    y_sample = xs.reshape(DB, DS, D_MODEL)
    return (y_prompt, y_sample,
            jnp.stack(kp_l), jnp.stack(vp_l), jnp.stack(ikp_l),
            jnp.stack(mkp_l), jnp.stack(mvp_l), jnp.stack(srp_l), jnp.stack(sip_l),
            jnp.stack(ks_l), jnp.stack(vs_l), jnp.stack(iks_l), jnp.stack(srs_l), jnp.stack(sis_l))
```

```python
import functools
import math

import jax
import jax.numpy as jnp
from jax import lax
from jax.experimental import pallas as pl
from jax.experimental.pallas import tpu as pltpu

F32 = jnp.float32
BF16 = jnp.bfloat16
I32 = jnp.int32

D_MODEL = 2048
D_SSM = 1024
SSM_GROUP = 16
SSM_GROUPS = 64
SSM_STATE = 64
N_HEADS = 8
HEAD_DIM = 128
D_ATT = 1024
N_KV_HEADS = 4
GQA = 2
D_KV = 512
IDX_HEADS = 16
IDX_DIM = 64
TOPK_MAX = 256
N_MEM = 256
MEM_HEADS = 4
MEM_HEAD_DIM = 128
D_MEM = 512
PAGE_SIZE = 128
ROPE_THETA = 10000.0
RMS_EPS = 1e-6
IW_SCALE = (IDX_HEADS * IDX_DIM) ** -0.5
ATT_SCALE = HEAD_DIM ** -0.5
MEM_SCALE = MEM_HEAD_DIM ** -0.5
LOG2E = 1.0 / math.log(2.0)
NEG_BIG = -1e30
INT_MIN = -2 ** 31
KEY_NEG_INF = (0xFF800000 ^ 0x7FFFFFFF) - 2 ** 32

LANES = 128
SUBLANES = 8
PACKED_ROWS = 16
VMEM_LIMIT = 56 * 1024 * 1024

OFF_U, OFF_ZS, OFF_Q, OFF_K, OFF_V, OFF_ZA, OFF_IQ = 0, 1024, 2048, 3072, 3584, 4096, 5120
OFF_MQ, OFF_ZM, OFF_G = 6144, 6656, 7168
N_MAIN = 13312
N_STATE = 2 * SSM_GROUPS * SSM_STATE
N_STILE = N_STATE // LANES
N_RTILE = N_STILE // 2


def _cparams(*sem):
    return pltpu.CompilerParams(dimension_semantics=sem, vmem_limit_bytes=VMEM_LIMIT)


def _sigmoid(x):
    return 0.5 * jnp.tanh(0.5 * x) + 0.5


def _silu(x):
    return x * _sigmoid(x)


def _split3(x):
    hi = x.astype(BF16)
    rem = x - hi.astype(F32)
    mid = rem.astype(BF16)
    lo = (rem - mid.astype(F32)).astype(BF16)
    return hi, mid, lo


def _pieces_a(x):
    h, m, l = (p.astype(F32) for p in _split3(x))
    b1 = h + pltpu.roll(m, IDX_DIM, 1)
    b3 = h + pltpu.roll(l, IDX_DIM, 1)
    return jnp.concatenate([b1, b1, b3], axis=1).astype(BF16)


def _pieces_b_rows(x):
    h, m, l = _split3(x)
    return jnp.concatenate([h, h, m, m, l, h], axis=0)


def _nt_dot(a, b):
    return lax.dot_general(a, b, (((1,), (1,)), ((), ())), preferred_element_type=F32)


def _rope128(acc, c, s):
    outs = []
    for k in range(acc.shape[1] // LANES):
        x = acc[:, k * LANES:(k + 1) * LANES]
        outs.append(x * c + pltpu.roll(x, 64, 1) * s)
    return outs[0] if len(outs) == 1 else jnp.concatenate(outs, axis=1)


def _rope64(acc, c, s):
    lane = lax.broadcasted_iota(I32, (1, LANES), 1)
    first = (lane % 64) < 32
    outs = []
    for k in range(acc.shape[1] // LANES):
        x = acc[:, k * LANES:(k + 1) * LANES]
        rot = jnp.where(first, pltpu.roll(x, 96, 1), pltpu.roll(x, 32, 1))
        outs.append(x * c + rot * s)
    return outs[0] if len(outs) == 1 else jnp.concatenate(outs, axis=1)


def _norm_matmul_kernel(x_ref, g_ref, w_ref, ca_ref, sa_ref, cb_ref, sb_ref, o_ref, xn_ref,
                        *, tiles128, tiles64):
    j = pl.program_id(1)

    @pl.when(j == 0)
    def _():
        x = x_ref[...]
        ms = jnp.mean(x * x, axis=-1, keepdims=True)
        xn_ref[...] = (x * lax.rsqrt(ms + RMS_EPS) * g_ref[...]).astype(BF16)

    def matmul():
        return jnp.dot(xn_ref[...], w_ref[...], preferred_element_type=F32)

    plain = None
    for tile, ncols in tiles128 + tiles64:
        hit = j == tile
        plain = jnp.logical_not(hit) if plain is None else plain & jnp.logical_not(hit)

        @pl.when(hit)
        def _(tile=tile, ncols=ncols):
            acc = matmul()
            if (tile, ncols) in tiles128:
                o_ref[:, :ncols] = _rope128(acc[:, :ncols], ca_ref[...], sa_ref[...])
            else:
                o_ref[:, :ncols] = _rope64(acc[:, :ncols], cb_ref[...], sb_ref[...])
            if ncols < acc.shape[1]:
                o_ref[:, ncols:] = acc[:, ncols:]

    if plain is None:
        o_ref[...] = matmul()
    else:
        @pl.when(plain)
        def _():
            o_ref[...] = matmul()


def norm_matmul(x, g, w, tabs, *, tm, tn, tiles128=(), tiles64=()):
    M, K = x.shape
    N = w.shape[1]
    ca, sa, cb, sb = tabs
    nt = ca.shape[0] // tm
    tab_spec = pl.BlockSpec((tm, LANES), lambda i, j: (i % nt, 0))
    return pl.pallas_call(
        functools.partial(_norm_matmul_kernel, tiles128=tuple(tiles128), tiles64=tuple(tiles64)),
        grid=(M // tm, N // tn),
        in_specs=[
            pl.BlockSpec((tm, K), lambda i, j: (i, 0)),
            pl.BlockSpec((1, K), lambda i, j: (0, 0)),
            pl.BlockSpec((K, tn), lambda i, j: (0, j)),
            tab_spec, tab_spec, tab_spec, tab_spec,
        ],
        out_specs=pl.BlockSpec((tm, tn), lambda i, j: (i, j)),
        out_shape=jax.ShapeDtypeStruct((M, N), F32),
        scratch_shapes=[pltpu.VMEM((tm, K), BF16)],
        compiler_params=_cparams("parallel", "arbitrary"),
        name="norm_matmul",
    )(x, g.reshape(1, K), w, ca, sa, cb, sb)


def _ssm_kernel(u_ref, zs_ref, s0_ref, a_ref, wb_ref, wc_ref, d_ref, wglu_ref, o_ref, sn_ref,
                bh_ref, st_ref, *, nb, tc, lt):
    j = pl.program_id(0)
    rows = nb * tc

    @pl.when(j == 0)
    def _():
        st_ref[...] = s0_ref[...]

    u = u_ref[...].reshape(rows, D_SSM)
    ri = lax.broadcasted_iota(I32, (rows, 1), 0)
    ci = lax.broadcasted_iota(I32, (1, rows), 1)
    to_tb = (ci == (ri % nb) * tc + ri // nb).astype(BF16)
    to_bt = (ci == (ri % tc) * nb + ri // tc).astype(BF16)
    ub = jnp.dot(to_tb, u.astype(BF16), preferred_element_type=F32).astype(BF16)
    for n in range(8):
        res = jnp.dot(ub[:, n * LANES:(n + 1) * LANES], wb_ref[n], preferred_element_type=F32)
        for k in range(4):
            bh_ref[n * 4 + k] = res[:, k * LANES:(k + 1) * LANES]
            bh_ref[N_RTILE + n * 4 + k] = res[:, 512 + k * LANES:512 + (k + 1) * LANES]

    sub = lax.broadcasted_iota(I32, (SUBLANES, LANES), 0)
    low = sub < 4
    for c in range(N_RTILE // lt):
        tr = [c * lt + k for k in range(lt)]
        ti = [N_RTILE + c * lt + k for k in range(lt)]
        ar = [a_ref[t] for t in tr]
        ai = [a_ref[t] for t in ti]
        h0 = tuple(st_ref[t] for t in tr), tuple(st_ref[t] for t in ti)

        def cstep(hr, hi, k, br, bi):
            return ar[k] * hr - ai[k] * hi + br, ar[k] * hi + ai[k] * hr + bi

        if nb == SUBLANES:
            def step(t, carry):
                hr, hi = carry
                off = pl.multiple_of(t * SUBLANES, SUBLANES)
                nr, ni = [], []
                for k in range(lt):
                    r, i = cstep(hr[k], hi[k], k, bh_ref[tr[k], pl.ds(off, SUBLANES), :],
                                 bh_ref[ti[k], pl.ds(off, SUBLANES), :])
                    bh_ref[tr[k], pl.ds(off, SUBLANES), :] = r
                    bh_ref[ti[k], pl.ds(off, SUBLANES), :] = i
                    nr.append(r)
                    ni.append(i)
                return tuple(nr), tuple(ni)

            hN = lax.fori_loop(0, tc, step, h0)
        else:
            def step(t, carry):
                hr, hi = carry
                off = pl.multiple_of(t * SUBLANES, SUBLANES)
                nr, ni = [], []
                for k in range(lt):
                    br = bh_ref[tr[k], pl.ds(off, SUBLANES), :]
                    bi = bh_ref[ti[k], pl.ds(off, SUBLANES), :]
                    r1, i1 = cstep(hr[k], hi[k], k, br, bi)
                    r2, i2 = cstep(pltpu.roll(r1, 4, 0), pltpu.roll(i1, 4, 0), k, br, bi)
                    bh_ref[tr[k], pl.ds(off, SUBLANES), :] = jnp.where(low, r1, r2)
                    bh_ref[ti[k], pl.ds(off, SUBLANES), :] = jnp.where(low, i1, i2)
                    nr.append(pltpu.roll(r2, 4, 0))
                    ni.append(pltpu.roll(i2, 4, 0))
                return tuple(nr), tuple(ni)

            hN = lax.fori_loop(0, rows // SUBLANES, step, h0, unroll=2)
        for k in range(lt):
            st_ref[tr[k]] = hN[0][k]
            st_ref[ti[k]] = hN[1][k]

    sn_ref[...] = st_ref[...]

    ys = []
    for n in range(8):
        hcat = jnp.concatenate([bh_ref[n * 4 + k] for k in range(4)]
                               + [bh_ref[N_RTILE + n * 4 + k] for k in range(4)], axis=1)
        ys.append(jnp.dot(hcat.astype(BF16), wc_ref[n], preferred_element_type=F32))
    yc = jnp.concatenate(ys, axis=1)
    y_hi, y_mid, y_lo = _split3(yc)
    yc = (jnp.dot(to_bt, y_hi, preferred_element_type=F32)
          + jnp.dot(to_bt, y_mid, preferred_element_type=F32)
          + jnp.dot(to_bt, y_lo, preferred_element_type=F32))
    y = yc + d_ref[...] * u
    y = jax.nn.gelu(y, approximate=True)
    proj = jnp.dot(y.astype(BF16), wglu_ref[...], preferred_element_type=F32)
    y = proj[:, :D_SSM] * _sigmoid(proj[:, D_SSM:])
    out = (y * _silu(zs_ref[...].reshape(rows, D_SSM))).astype(BF16)
    o_ref[...] = out.reshape(o_ref.shape)


def ssm_branch(h, s0, a_t, wb, wc, d, wglu, *, nb, tc):
    M = h.shape[0]
    T = M // nb
    rows = nb * tc
    const3 = lambda j: (0, 0, 0)
    if T == tc:
        hv = h
        u_spec = pl.BlockSpec((rows, D_SSM), lambda j: (0, OFF_U // D_SSM))
        z_spec = pl.BlockSpec((rows, D_SSM), lambda j: (0, OFF_ZS // D_SSM))
        o_spec = pl.BlockSpec((rows, D_SSM), lambda j: (0, 0))
        o_shape = jax.ShapeDtypeStruct((M, D_SSM), BF16)
    else:
        hv = h.reshape(nb, T, h.shape[1])
        u_spec = pl.BlockSpec((nb, tc, D_SSM), lambda j: (0, j, OFF_U // D_SSM))
        z_spec = pl.BlockSpec((nb, tc, D_SSM), lambda j: (0, j, OFF_ZS // D_SSM))
        o_spec = pl.BlockSpec((nb, tc, D_SSM), lambda j: (0, j, 0))
        o_shape = jax.ShapeDtypeStruct((nb, T, D_SSM), BF16)
    out, st = pl.pallas_call(
        functools.partial(_ssm_kernel, nb=nb, tc=tc, lt=4),
        grid=(T // tc,),
        in_specs=[
            u_spec,
            z_spec,
            pl.BlockSpec((N_STILE, SUBLANES, LANES), const3),
            pl.BlockSpec((N_STILE, SUBLANES, LANES), const3),
            pl.BlockSpec((8, LANES, 1024), const3),
            pl.BlockSpec((8, 1024, LANES), const3),
            pl.BlockSpec((1, D_SSM), lambda j: (0, 0)),
            pl.BlockSpec((D_SSM, 2 * D_SSM), lambda j: (0, 0)),
        ],
        out_specs=[
            o_spec,
            pl.BlockSpec((N_STILE, SUBLANES, LANES), const3),
        ],
        out_shape=[
            o_shape,
            jax.ShapeDtypeStruct((N_STILE, SUBLANES, LANES), F32),
        ],
        scratch_shapes=[
            pltpu.VMEM((N_STILE, rows, LANES), F32),
            pltpu.VMEM((N_STILE, SUBLANES, LANES), F32),
        ],
        compiler_params=_cparams("arbitrary"),
        name="ssm_branch",
    )(hv, hv, s0, a_t, wb, wc, d.reshape(1, D_SSM), wglu)
    return out.reshape(M, D_SSM), st


def _sum_lanes(x):
    w = x.shape[1]
    while w > LANES and (w // 2) % LANES == 0:
        x = x[:, :w // 2] + x[:, w // 2:]
        w //= 2
    return jnp.sum(x, axis=-1, keepdims=True)


def _reduce_rows(x, op):
    n, lanes = x.shape
    slab = 8 * SUBLANES
    if n > slab and n % slab == 0:
        x = op(x.reshape(n // slab, slab, lanes), axis=0)
    return op(x, axis=0, keepdims=True)


def _sort_key(score):
    bits = pltpu.bitcast(score, I32)
    return jnp.where(bits < 0, bits ^ jnp.int32(0x7FFFFFFF), bits)


def _key_score(key):
    key = jnp.maximum(key, jnp.int32(KEY_NEG_INF))
    return pltpu.bitcast(jnp.where(key < 0, key ^ jnp.int32(0x7FFFFFFF), key), F32)


def _kth_largest_key(count_ge, shape, k):
    thr = jnp.full(shape, INT_MIN, I32)
    for bit in range(31, -1, -1):
        inc = INT_MIN if bit == 31 else (1 << bit)
        cand = thr + jnp.int32(inc)
        thr = jnp.where(count_ge(cand) >= k, cand, thr)
    return thr


def _count_rows16(x, cand, strict=False):
    n = x.shape[0]
    m = jnp.where((x > cand) if strict else (x >= cand), jnp.int16(1), jnp.int16(0))
    slab = 8 * PACKED_ROWS
    acc = m[0:slab]
    for i in range(1, n // slab):
        acc = acc + m[i * slab:(i + 1) * slab]
    return jnp.sum(acc.astype(I32), axis=0, keepdims=True)


def _kth_largest16(x, k):
    thr = jnp.full((1, x.shape[1]), -(1 << 15), I32)
    for bit in range(15, -1, -1):
        cand = thr + jnp.int32(1 << bit)
        thr = jnp.where(_count_rows16(x, cand.astype(jnp.int16)) >= k, cand, thr)
    return thr


def _kth_largest_key_cols(key, k):
    hi = (key >> 16).astype(jnp.int16)
    lo = ((key & 0xFFFF) - (1 << 15)).astype(jnp.int16)
    top = _kth_largest16(hi, k)
    top16 = top.astype(jnp.int16)
    k_in_bucket = k - _count_rows16(hi, top16, strict=True)
    lo_in_bucket = jnp.where(hi == top16, lo, jnp.int16(-(1 << 15)))
    low = _kth_largest16(lo_in_bucket, k_in_bucket)
    n_ge = (k - k_in_bucket) + _count_rows16(lo_in_bucket, low.astype(jnp.int16))
    return top * (1 << 16) + (low + (1 << 15)), n_ge


def _attend_block(q_ref, za_ref, kb_ref, vt_ref, o_ref, sel, nk):
    for kvh in range(N_KV_HEADS):
        hs = slice(kvh * HEAD_DIM, (kvh + 1) * HEAD_DIM)
        kh = kb_ref[0:nk, hs]
        vt = vt_ref[hs, 0:nk]
        for g in range(GQA):
            cs = slice((kvh * GQA + g) * HEAD_DIM, (kvh * GQA + g + 1) * HEAD_DIM)
            lg = jnp.where(sel, _nt_dot(kh, q_ref[:, cs].astype(BF16)), NEG_BIG)
            m = _reduce_rows(lg, jnp.max)
            p = jnp.exp2((lg - m) * (ATT_SCALE * LOG2E))
            l = _reduce_rows(p, jnp.sum)
            o_t = jnp.dot(vt, p.astype(BF16), preferred_element_type=F32) / l
            o_ref[:, cs] = (jnp.transpose(o_t) * _silu(za_ref[:, cs])).astype(BF16)


def _dsa_prompt_body(q_ref, za_ref, iq_ref, iwq_ref, kb_ref, vt_ref, ik6_ref, o_ref, key_ref, thr_ref,
                     exc_ref, sc_ref, qi, *, tq, topk, nk):
    iw_t = jnp.transpose(iwq_ref[...]) * IW_SCALE
    iq = iq_ref[...].astype(BF16)
    ik = ik6_ref[0:nk, :]
    sc = jnp.zeros((nk, tq), F32)
    for h in range(IDX_HEADS):
        s = _nt_dot(ik, iq[:, h * IDX_DIM:(h + 1) * IDX_DIM])
        sc = sc + jnp.maximum(s, 0.0) * iw_t[IDX_DIM + h:IDX_DIM + h + 1, :]
    kpos = lax.broadcasted_iota(I32, (nk, 1), 0)
    qpos = qi * tq + lax.broadcasted_iota(I32, (1, tq), 1)
    causal = kpos <= qpos
    sc_ref[0:nk, :] = jnp.where(causal, sc, -jnp.inf)
    key_ref[0:nk, :] = _sort_key(sc_ref[0:nk, :])

    def count_ge(cand):
        return _reduce_rows((sc_ref[0:nk, :] >= _key_score(cand)).astype(F32), jnp.sum)

    thr = _kth_largest_key(count_ge, (1, tq), float(topk))
    n_ge = count_ge(thr).astype(I32)
    thr_ref[...] = jnp.broadcast_to(thr, thr_ref.shape)
    exc_ref[...] = jnp.broadcast_to(jnp.where(thr == KEY_NEG_INF, 0, n_ge - topk), exc_ref.shape)
    sel = (sc_ref[0:nk, :] >= _key_score(thr)) & causal
    _attend_block(q_ref, za_ref, kb_ref, vt_ref, o_ref, sel, nk)


def _dsa_tie_path(q_ref, za_ref, kb_ref, vt_ref, o_ref, key_ref, thr_ref, sel_ref, qi, *, tq, topk, nk):
    thr = thr_ref[0:1, :]
    qpos = qi * tq + lax.broadcasted_iota(I32, (1, tq), 1)
    kpos = lax.broadcasted_iota(I32, (nk, 1), 0)
    causal = kpos <= qpos
    need = topk - _reduce_rows(((key_ref[0:nk, :] > thr) & causal).astype(F32), jnp.sum)
    ri = lax.broadcasted_iota(I32, (LANES, LANES), 0)
    ci = lax.broadcasted_iota(I32, (LANES, LANES), 1)
    before = (ci < ri).astype(BF16)
    carry = jnp.zeros((1, tq), F32)
    for blk in range(nk // LANES):
        rs = slice(blk * LANES, (blk + 1) * LANES)
        kb = key_ref[rs, :]
        cz = causal[rs]
        tied = (kb == thr) & cz
        tied_f = jnp.where(tied, 1.0, 0.0)
        rank = jnp.dot(before, tied_f.astype(BF16), preferred_element_type=F32) + carry
        sel_ref[rs, :] = (((kb > thr) & cz) | (tied & (rank < need))).astype(I32)
        carry = carry + jnp.sum(tied_f, axis=0, keepdims=True)
    _attend_block(q_ref, za_ref, kb_ref, vt_ref, o_ref, sel_ref[0:nk, :] != 0, nk)


def _dsa_prompt_kernel(q_ref, za_ref, iq_ref, iwq_ref, k_ref, v_ref, ik_ref, o_ref,
                       key_ref, sel_ref, thr_ref, exc_ref, sc_ref, kb_ref, vt_ref, ik6_ref, *, tq, topk, kc):
    qi = pl.program_id(1)
    T = k_ref.shape[0]

    @pl.when(qi == 0)
    def _():
        kb_ref[...] = k_ref[...].astype(BF16)
        ik6_ref[...] = ik_ref[:, 0:IDX_DIM].astype(BF16)
        key_ref[...] = jnp.full(key_ref.shape, KEY_NEG_INF, I32)
        for c in range(T // LANES):
            rs = slice(c * LANES, (c + 1) * LANES)
            for kvh in range(N_KV_HEADS):
                hs = slice(kvh * HEAD_DIM, (kvh + 1) * HEAD_DIM)
                vt_ref[hs, rs] = jnp.transpose(v_ref[rs, hs]).astype(BF16)

    need = (qi + 1) * tq
    prefixes = [p for p in (T // 4, T // 2) if p >= kc and p % kc == 0] + [T]
    for lo, hi in zip([0] + prefixes[:-1], prefixes):
        @pl.when((need > lo) & (need <= hi))
        def _(hi=hi):
            _dsa_prompt_body(q_ref, za_ref, iq_ref, iwq_ref, kb_ref, vt_ref, ik6_ref, o_ref, key_ref,
                             thr_ref, exc_ref, sc_ref, qi, tq=tq, topk=topk, nk=hi)

    @pl.when(jnp.max(exc_ref[...]) > 0)
    def _():
        _dsa_tie_path(q_ref, za_ref, kb_ref, vt_ref, o_ref, key_ref, thr_ref, sel_ref, qi,
                      tq=tq, topk=topk, nk=T)


def dsa_prompt(h, ikw, *, nb, tq, kc=128):
    T = h.shape[0] // nb
    topk = min(TOPK_MAX, T // 4)
    nq = T // tq
    return pl.pallas_call(
        functools.partial(_dsa_prompt_kernel, tq=tq, topk=topk, kc=kc),
        grid=(nb, nq),
        in_specs=[
            pl.BlockSpec((tq, D_ATT), lambda b, i: (b * nq + i, OFF_Q // D_ATT)),
            pl.BlockSpec((tq, D_ATT), lambda b, i: (b * nq + i, OFF_ZA // D_ATT)),
            pl.BlockSpec((tq, D_ATT), lambda b, i: (b * nq + i, OFF_IQ // D_ATT)),
            pl.BlockSpec((tq, LANES), lambda b, i: (b * nq + i, 0)),
            pl.BlockSpec((T, D_KV), lambda b, i: (b, OFF_K // D_KV)),
            pl.BlockSpec((T, D_KV), lambda b, i: (b, OFF_V // D_KV)),
            pl.BlockSpec((T, LANES), lambda b, i: (b, 0)),
        ],
        out_specs=pl.BlockSpec((tq, D_ATT), lambda b, i: (b * nq + i, 0)),
        out_shape=jax.ShapeDtypeStruct((nb * T, D_ATT), BF16),
        scratch_shapes=[
            pltpu.VMEM((T, tq), I32),
            pltpu.VMEM((T, tq), I32),
            pltpu.VMEM((SUBLANES, tq), I32),
            pltpu.VMEM((SUBLANES, tq), I32),
            pltpu.VMEM((T, tq), F32),
            pltpu.VMEM((T, D_KV), BF16),
            pltpu.VMEM((D_KV, T), BF16),
            pltpu.VMEM((T, IDX_DIM), BF16),
        ],
        compiler_params=_cparams("parallel", "arbitrary"),
        name="dsa_prompt",
    )(h, h, h, ikw, h, h, ikw)


def _mem_attn_kernel(q_ref, z_ref, k_ref, v_ref, o_ref):
    for hh in range(MEM_HEADS):
        cs = slice(hh * MEM_HEAD_DIM, (hh + 1) * MEM_HEAD_DIM)
        lg = _nt_dot(q_ref[:, cs].astype(BF16), k_ref[:, cs].astype(BF16)) * MEM_SCALE
        m = jnp.max(lg, axis=-1, keepdims=True)
        p = jnp.exp(lg - m)
        l = jnp.sum(p, axis=-1, keepdims=True)
        o = jnp.dot(p.astype(BF16), v_ref[:, cs].astype(BF16), preferred_element_type=F32) / l
        o_ref[:, cs] = (o * _silu(z_ref[:, cs])).astype(BF16)


def mem_attn_prompt(h, mkv, *, nb, tq):
    T = h.shape[0] // nb
    nq = T // tq
    return pl.pallas_call(
        _mem_attn_kernel,
        grid=(nb, nq),
        in_specs=[
            pl.BlockSpec((tq, D_MEM), lambda b, i: (b * nq + i, OFF_MQ // D_MEM)),
            pl.BlockSpec((tq, D_MEM), lambda b, i: (b * nq + i, OFF_ZM // D_MEM)),
            pl.BlockSpec((N_MEM, D_MEM), lambda b, i: (b, 0)),
            pl.BlockSpec((N_MEM, D_MEM), lambda b, i: (b, 1)),
        ],
        out_specs=pl.BlockSpec((tq, D_MEM), lambda b, i: (b * nq + i, 0)),
        out_shape=jax.ShapeDtypeStruct((nb * T, D_MEM), BF16),
        compiler_params=_cparams("parallel", "arbitrary"),
        name="mem_attn_prompt",
    )(h, h, mkv, mkv)


def mem_attn_sample(mq, zm, mk, mv):
    B, R, _ = mq.shape
    qspec = pl.BlockSpec((None, R, D_MEM), lambda b: (b, 0, 0))
    kspec = pl.BlockSpec((None, N_MEM, D_MEM), lambda b: (b, 0, 0))
    return pl.pallas_call(
        _mem_attn_kernel,
        grid=(B,),
        in_specs=[qspec, qspec, kspec, kspec],
        out_specs=qspec,
        out_shape=jax.ShapeDtypeStruct((B, R, D_MEM), BF16),
        compiler_params=_cparams("parallel"),
        name="mem_attn_sample",
    )(mq, zm, mk, mv)


def _merge_kernel(as_ref, aa_ref, am_ref, gs_ref, ga_ref, gm_ref, ws_ref, wa_ref, wm_ref, o_ref):
    ys = jnp.dot(as_ref[...], ws_ref[...], preferred_element_type=F32)
    ya = jnp.dot(aa_ref[...], wa_ref[...], preferred_element_type=F32)
    ym = jnp.dot(am_ref[...], wm_ref[...], preferred_element_type=F32)
    o = _sigmoid(gs_ref[...]) * ys + _sigmoid(ga_ref[...]) * ya + _sigmoid(gm_ref[...]) * ym
    o_ref[...] = o.astype(BF16)


def merge_branches(a_ssm, a_att, a_mem, h, w_ssm, w_att, w_mem, *, tm, tn=1024):
    M = h.shape[0]
    gb = OFF_G // tn
    per = D_MODEL // tn
    return pl.pallas_call(
        _merge_kernel,
        grid=(per, M // tm),
        in_specs=[
            pl.BlockSpec((tm, D_SSM), lambda n, i: (i, 0)),
            pl.BlockSpec((tm, D_ATT), lambda n, i: (i, 0)),
            pl.BlockSpec((tm, D_MEM), lambda n, i: (i, 0)),
            pl.BlockSpec((tm, tn), lambda n, i: (i, gb + n)),
            pl.BlockSpec((tm, tn), lambda n, i: (i, gb + per + n)),
            pl.BlockSpec((tm, tn), lambda n, i: (i, gb + 2 * per + n)),
            pl.BlockSpec((D_SSM, tn), lambda n, i: (0, n)),
            pl.BlockSpec((D_ATT, tn), lambda n, i: (0, n)),
            pl.BlockSpec((D_MEM, tn), lambda n, i: (0, n)),
        ],
        out_specs=pl.BlockSpec((tm, tn), lambda n, i: (i, n)),
        out_shape=jax.ShapeDtypeStruct((M, D_MODEL), BF16),
        compiler_params=_cparams("parallel", "arbitrary"),
        name="merge_branches",
    )(a_ssm, a_att, a_mem, h, h, h, w_ssm, w_att, w_mem)


def _out_kernel(m_ref, w_ref, x_ref, g_ref, o_ref, *, final_norm):
    y = x_ref[...] + jnp.dot(m_ref[...], w_ref[...], preferred_element_type=F32)
    if final_norm:
        ms = jnp.mean(y * y, axis=-1, keepdims=True)
        y = y * lax.rsqrt(ms + RMS_EPS) * g_ref[...]
    o_ref[...] = y


def out_proj(merged, w_o, x, g, *, tm, final_norm):
    M = x.shape[0]
    return pl.pallas_call(
        functools.partial(_out_kernel, final_norm=final_norm),
        grid=(M // tm,),
        in_specs=[
            pl.BlockSpec((tm, D_MODEL), lambda i: (i, 0)),
            pl.BlockSpec((D_MODEL, D_MODEL), lambda i: (0, 0)),
            pl.BlockSpec((tm, D_MODEL), lambda i: (i, 0)),
            pl.BlockSpec((1, D_MODEL), lambda i: (0, 0)),
        ],
        out_specs=pl.BlockSpec((tm, D_MODEL), lambda i: (i, 0)),
        out_shape=jax.ShapeDtypeStruct((M, D_MODEL), F32),
        compiler_params=_cparams("parallel"),
        name="out_proj",
    )(merged, w_o, x, g.reshape(1, D_MODEL))


def _page_scores(iq_ref, iwb_ref, pages_t):
    kt = pages_t[0] if len(pages_t) == 1 else jnp.concatenate(pages_t, axis=1)
    s = jnp.dot(_pieces_a(iq_ref[...]), _pieces_b_rows(kt), preferred_element_type=F32)
    iwb = iwb_ref[...]
    outs = []
    for i in range(len(pages_t)):
        w = jnp.maximum(s[:, i * PAGE_SIZE:(i + 1) * PAGE_SIZE], 0.0) * iwb
        outs.append(jnp.sum(w.reshape(IDX_HEADS, SUBLANES, PAGE_SIZE), axis=0))
    return outs


def _idx_sample_kernel(pt_ref, iq_ref, iwb_ref, *rest, pp):
    pages, o_ref = rest[:pp], rest[pp]
    outs = _page_scores(iq_ref, iwb_ref, [pages[i][...] for i in range(pp)])
    for i in range(pp):
        o_ref[:, i * PAGE_SIZE:(i + 1) * PAGE_SIZE] = outs[i]


def idx_scores_sample(kidx_t, layer, pt_flat, iqm, iwb, *, n_pages, pp):
    B = iqm.shape[0]

    def page_spec(i):
        return pl.BlockSpec((None, None, IDX_DIM, PAGE_SIZE),
                            lambda b, s, pt: (layer, pt[b * n_pages + s * pp + i], 0, 0))

    grid_spec = pltpu.PrefetchScalarGridSpec(
        num_scalar_prefetch=1,
        grid=(B, n_pages // pp),
        in_specs=[
            pl.BlockSpec((None,) + iqm.shape[1:], lambda b, s, pt: (b, 0, 0)),
            pl.BlockSpec((None,) + iwb.shape[1:], lambda b, s, pt: (b, 0, 0)),
        ] + [page_spec(i) for i in range(pp)],
        out_specs=pl.BlockSpec((None, SUBLANES, pp * PAGE_SIZE), lambda b, s, pt: (b, 0, s)),
    )
    return pl.pallas_call(
        functools.partial(_idx_sample_kernel, pp=pp),
        grid_spec=grid_spec,
        out_shape=jax.ShapeDtypeStruct((B, SUBLANES, n_pages * PAGE_SIZE), F32),
        compiler_params=_cparams("parallel", "arbitrary"),
        name="idx_scores_sample",
    )(pt_flat, iqm, iwb, *([kidx_t] * pp))


def _attn_sample_kernel(pt_ref, sc_ref, iq_ref, iwb_ref, iknew_ref, q_ref, knew_ref, vnew_ref, *rest,
                        pp, ds, topk):
    kpages, vpages = rest[:pp], rest[pp:2 * pp]
    o_ref = rest[2 * pp]
    key_ref, keyn_ref, m_ref, l_ref, acc_ref = rest[2 * pp + 1:]
    s = pl.program_id(1)
    ns = pl.num_programs(1)

    @pl.when(s == 0)
    def _():
        sn = _page_scores(iq_ref, iwb_ref, [iknew_ref[...]])[0]
        row = lax.broadcasted_iota(I32, sn.shape, 0) % ds
        col = lax.broadcasted_iota(I32, sn.shape, 1)
        sn = jnp.where(col <= row, sn, -jnp.inf)
        keyn_ref[...] = _sort_key(sn)
        key_ref[...] = _sort_key(sc_ref[...])

        def count(pred_old, pred_new):
            return (_sum_lanes(pred_old.astype(F32)) + jnp.sum(pred_new.astype(F32), axis=-1, keepdims=True))

        def count_ge(cand):
            cf = _key_score(cand)
            return count(sc_ref[...] >= cf, sn >= cf)

        thr = _kth_largest_key(count_ge, (SUBLANES, 1), float(topk))
        tf = _key_score(thr)
        excess = count_ge(thr) - topk
        need = topk - count(sc_ref[...] > tf, sn > tf)
        tiles = [key_ref.at[:, pl.ds(j * LANES, LANES)] for j in range(key_ref.shape[1] // LANES)]
        tiles.append(keyn_ref)

        @pl.when(jnp.max(excess) <= 0.0)
        def _():
            key_ref[...] = (sc_ref[...] >= tf).astype(I32)
            keyn_ref[...] = (sn >= tf).astype(I32)

        @pl.when(jnp.max(excess) > 0.0)
        def _():
            ri = lax.broadcasted_iota(I32, (LANES, LANES), 0)
            ci = lax.broadcasted_iota(I32, (LANES, LANES), 1)
            before = (ri < ci).astype(BF16)
            carry = jnp.zeros((SUBLANES, 1), F32)
            for t in tiles:
                kt = t[...]
                tied = kt == thr
                tied_f = jnp.where(tied, 1.0, 0.0)
                rank = jnp.dot(tied_f.astype(BF16), before, preferred_element_type=F32) + carry
                t[...] = ((kt > thr) | (tied & (rank < need))).astype(I32)
                carry = carry + jnp.sum(tied_f, axis=-1, keepdims=True)

        m_ref[...] = jnp.full(m_ref.shape, NEG_BIG, F32)
        l_ref[...] = jnp.zeros(l_ref.shape, F32)
        acc_ref[...] = jnp.zeros(acc_ref.shape, F32)

    qrows = N_KV_HEADS * SUBLANES
    prows = PAGE_SIZE * N_KV_HEADS
    er = lax.broadcasted_iota(I32, (PAGE_SIZE, prows), 0)
    ec = lax.broadcasted_iota(I32, (PAGE_SIZE, prows), 1)
    expand = (ec // N_KV_HEADS == er).astype(BF16)
    rkv = lax.broadcasted_iota(I32, (qrows, prows), 0) // SUBLANES
    ckv = lax.broadcasted_iota(I32, (qrows, prows), 1) % N_KV_HEADS
    same_head = rkv == ckv

    def masked_logits(kpage, sel_tile):
        sel = jnp.where(sel_tile != 0, 1.0, 0.0).astype(BF16)
        selx = jnp.dot(sel, expand, preferred_element_type=F32)
        selx = jnp.concatenate([selx] * N_KV_HEADS, axis=0)
        lg = _nt_dot(q_ref[...], kpage.astype(BF16))
        return jnp.where((selx > 0.5) & same_head, lg, NEG_BIG)

    def update(lgs, vs):
        lg = lgs[0] if len(lgs) == 1 else jnp.concatenate(lgs, axis=1)
        m_old = m_ref[...]
        m_new = jnp.maximum(m_old, jnp.max(lg, axis=-1, keepdims=True))
        alpha = jnp.exp2((m_old - m_new) * (ATT_SCALE * LOG2E))
        p = jnp.exp2((lg - m_new[:, 0:1]) * (ATT_SCALE * LOG2E))
        l_ref[...] = alpha * l_ref[...] + jnp.sum(p, axis=-1, keepdims=True)
        pb = p.astype(BF16)
        pv = jnp.dot(pb[:, 0:prows], vs[0].astype(BF16), preferred_element_type=F32)
        for i in range(1, len(vs)):
            pv = pv + jnp.dot(pb[:, i * prows:(i + 1) * prows], vs[i].astype(BF16),
                              preferred_element_type=F32)
        acc_ref[...] = alpha * acc_ref[...] + pv
        m_ref[...] = m_new

    base = pl.multiple_of(s * (pp * PAGE_SIZE), pp * PAGE_SIZE)
    update([masked_logits(kpages[i][...], key_ref[:, pl.ds(base + i * PAGE_SIZE, PAGE_SIZE)])
            for i in range(pp)],
           [vpages[i][...] for i in range(pp)])

    @pl.when(s == ns - 1)
    def _():
        update([masked_logits(knew_ref[...], keyn_ref[...])], [vnew_ref[...]])
        o_ref[...] = acc_ref[...] / l_ref[...]


def attn_sample(cache_k, cache_v, layer, pt_flat, scores, iqm, iwb, iknew, qm, knew, vnew,
                *, n_pages, pp, ds):
    B = qm.shape[0]
    L = n_pages * PAGE_SIZE
    topk = min(TOPK_MAX, (L + ds) // 4)

    prows = PAGE_SIZE * N_KV_HEADS
    qrows = N_KV_HEADS * SUBLANES

    def page_spec(i):
        return pl.BlockSpec((None, None, prows, HEAD_DIM),
                            lambda b, s, pt: (layer, pt[b * n_pages + s * pp + i], 0, 0))

    def bspec(a):
        nd = a.ndim - 1
        return pl.BlockSpec((None,) + a.shape[1:], lambda b, s, pt: (b,) + (0,) * nd)

    grid_spec = pltpu.PrefetchScalarGridSpec(
        num_scalar_prefetch=1,
        grid=(B, n_pages // pp),
        in_specs=[bspec(scores), bspec(iqm), bspec(iwb), bspec(iknew), bspec(qm), bspec(knew), bspec(vnew)]
        + [page_spec(i) for i in range(pp)] * 2,
        out_specs=pl.BlockSpec((None, qrows, HEAD_DIM), lambda b, s, pt: (b, 0, 0)),
        scratch_shapes=[
            pltpu.VMEM((SUBLANES, L), I32),
            pltpu.VMEM((SUBLANES, LANES), I32),
            pltpu.VMEM((qrows, LANES), F32),
            pltpu.VMEM((qrows, LANES), F32),
            pltpu.VMEM((qrows, HEAD_DIM), F32),
        ],
    )
    return pl.pallas_call(
        functools.partial(_attn_sample_kernel, pp=pp, ds=ds, topk=topk),
        grid_spec=grid_spec,
        out_shape=jax.ShapeDtypeStruct((B, qrows, HEAD_DIM), F32),
        compiler_params=_cparams("parallel", "arbitrary"),
        name="attn_sample",
    )(pt_flat, scores, iqm, iwb, iknew, qm, knew, vnew, *([cache_k] * pp), *([cache_v] * pp))


def _ssm_disc_kernel(ar_ref, ai_ref, ldt_ref, abr_ref, abi_ref, cr_ref, ci_ref):
    ar, ai = ar_ref[...], ai_ref[...]
    dt = jnp.exp(ldt_ref[...])
    mag = jnp.exp(dt * ar)
    abr = mag * jnp.cos(dt * ai)
    abi = mag * jnp.sin(dt * ai)
    den = ar * ar + ai * ai
    nr, ni = abr - 1.0, abi
    abr_ref[...] = abr
    abi_ref[...] = abi
    cr_ref[...] = (nr * ar + ni * ai) / den
    ci_ref[...] = (ni * ar - nr * ai) / den


def ssm_discretise(a_re, a_im, log_dt):
    G, P = a_re.shape
    ldt = jnp.broadcast_to(log_dt[:, None], (G, P))
    shp = jax.ShapeDtypeStruct((G, P), F32)
    return pl.pallas_call(_ssm_disc_kernel, out_shape=[shp] * 4, name="ssm_discretise")(a_re, a_im, ldt)


def _bbar_kernel(cr_ref, ci_ref, br_ref, bi_ref, or_ref, oi_ref):
    cr, ci, br, bi = cr_ref[...], ci_ref[...], br_ref[...], bi_ref[...]
    or_ref[...] = cr * br - ci * bi
    oi_ref[...] = cr * bi + ci * br


def ssm_bbar(coef_r, coef_i, b_re, b_im):
    G, P, C = b_re.shape
    cr = jnp.broadcast_to(coef_r[:, :, None], (G, P, C)).reshape(G, P * C)
    ci = jnp.broadcast_to(coef_i[:, :, None], (G, P, C)).reshape(G, P * C)
    shp = jax.ShapeDtypeStruct((G, P * C), F32)
    r, i = pl.pallas_call(_bbar_kernel, out_shape=[shp] * 2, name="ssm_bbar")(
        cr, ci, b_re.reshape(G, P * C), b_im.reshape(G, P * C))
    return r.reshape(G, P, C), i.reshape(G, P, C)


def _pack_w_in(w):
    a = OFF_IQ + IDX_HEADS * IDX_DIM
    b = a + IDX_DIM + IDX_HEADS
    main = jnp.concatenate([w[:, :a], w[:, b:]], axis=1).astype(BF16)
    pad = jnp.zeros((w.shape[0], LANES - IDX_DIM - IDX_HEADS), w.dtype)
    ikw = jnp.concatenate([w[:, a:b], pad], axis=1).astype(BF16)
    return main, ikw


def _rope_tables(pos):
    def tabs(half):
        inv = ROPE_THETA ** (-jnp.arange(half, dtype=F32) / half)
        ang = pos[:, None] * inv[None, :]
        return jnp.cos(ang), jnp.sin(ang)

    c, s = tabs(64)
    c128 = jnp.concatenate([c, c], axis=1)
    s128 = jnp.concatenate([-s, s], axis=1)
    c, s = tabs(32)
    c64 = jnp.concatenate([c, c, c, c], axis=1)
    s64 = jnp.concatenate([-s, s, -s, s], axis=1)
    lane = jnp.arange(LANES)[None, :]
    c64k = jnp.where(lane < IDX_DIM, c64, 1.0)
    s64k = jnp.where(lane < IDX_DIM, s64, 0.0)
    return (c128, s128, c64, s64), (c128, s128, c64k, s64k)


def _ssm_weights(abr, abi, bbar_r, bbar_i, c_re, c_im):
    a_t = jnp.concatenate([abr.reshape(N_RTILE, 1, LANES), abi.reshape(N_RTILE, 1, LANES)], axis=0)
    a_t = jnp.broadcast_to(a_t, (N_STILE, SUBLANES, LANES))
    eye = jnp.eye(8, dtype=F32)

    def bmat(bb):
        bb = bb.reshape(8, 8, SSM_STATE, SSM_GROUP)
        return jnp.einsum('ngpc,hg->nhcgp', bb, eye).reshape(8, LANES, 512)

    wb = jnp.concatenate([bmat(bbar_r), bmat(bbar_i)], axis=2).astype(BF16)

    def cmat(cc):
        cc = cc.reshape(8, 8, SSM_GROUP, SSM_STATE)
        return jnp.einsum('ngcp,hg->nhpgc', cc, eye).reshape(8, 512, LANES)

    wc = jnp.concatenate([cmat(c_re), -cmat(c_im)], axis=1).astype(BF16)
    return a_t, wb, wc


def _state_to_tiles(s_re, s_im):
    nb = s_re.shape[0]

    def t(s):
        s = s.reshape(nb, N_RTILE, LANES).transpose(1, 0, 2)
        return jnp.pad(s, ((0, 0), (0, SUBLANES - nb), (0, 0)))

    return jnp.concatenate([t(s_re), t(s_im)], axis=0)


def _tiles_to_state(st, nb):
    def t(s):
        return s[:, :nb].transpose(1, 0, 2).reshape(nb, SSM_GROUPS, SSM_STATE)

    return t(st[:N_RTILE]), t(st[N_RTILE:])


def _gate_kernel(y_ref, z_ref, o_ref):
    o_ref[...] = (y_ref[...] * _silu(z_ref[...])).astype(BF16)


def silu_gate(y, z):
    return pl.pallas_call(_gate_kernel, out_shape=jax.ShapeDtypeStruct(y.shape, BF16), name="silu_gate")(y, z)


def _project(x, g, w_main, w_ikw, tabs_main, tabs_ikw, tm):
    h = norm_matmul(x, g, w_main, tabs_main, tm=tm, tn=1024,
                    tiles128=((OFF_Q // 1024, 1024), (OFF_K // 1024, D_KV)), tiles64=((OFF_IQ // 1024, 1024),))
    ikw = norm_matmul(x, g, w_ikw, tabs_ikw, tm=tm, tn=LANES, tiles64=((0, LANES),))
    return h, ikw


def kernel(x_prompt, x_sample, cache_k, cache_v, cache_kidx, cache_mem_k, cache_mem_v, state_ssm_re, state_ssm_im, page_table, mem_prompt, rms_g, w_in, ssm_a_re, ssm_a_im, ssm_log_dt, ssm_b_re, ssm_b_im, ssm_c_re, ssm_c_im, ssm_d, w_glu, w_out_ssm, w_out_att, w_out_mem, mem_norm, w_mem_kv, w_o, final_norm):
    Bp, T, _ = x_prompt.shape
    DB, DS, _ = x_sample.shape
    depth = w_in.shape[0]
    n_pages = page_table.shape[1]
    past = n_pages * PAGE_SIZE
    pool = cache_k.shape[1]
    pp = min(16, n_pages)

    xp = x_prompt.reshape(Bp * T, D_MODEL)
    xs = x_sample.reshape(DB * DS, D_MODEL)
    tabs_p, tabs_pk = _rope_tables(jnp.arange(T, dtype=F32))
    tabs_s, tabs_sk = _rope_tables(jnp.tile(jnp.float32(past) + jnp.arange(DS, dtype=F32), DB))
    mem_rows = mem_prompt.reshape(Bp * N_MEM, D_MODEL)
    one_tab = jnp.ones((Bp * N_MEM, LANES), F32)
    pt_flat = page_table.reshape(-1).astype(I32)
    cache_kr = cache_k.reshape(depth, pool, PAGE_SIZE * N_KV_HEADS, HEAD_DIM)
    cache_vr = cache_v.reshape(depth, pool, PAGE_SIZE * N_KV_HEADS, HEAD_DIM)
    kidx_t = jnp.swapaxes(cache_kidx, 2, 3)
    zero_state = jnp.zeros((Bp, SSM_GROUPS, SSM_STATE), F32)

    kp_l, vp_l, ikp_l, mkp_l, mvp_l, srp_l, sip_l = [], [], [], [], [], [], []
    ks_l, vs_l, iks_l, srs_l, sis_l = [], [], [], [], []
    for l in range(depth):
        w_main, w_ikw = _pack_w_in(w_in[l])
        abr, abi, coef_r, coef_i = ssm_discretise(ssm_a_re[l], ssm_a_im[l], ssm_log_dt[l])
        bbar_r, bbar_i = ssm_bbar(coef_r, coef_i, ssm_b_re[l], ssm_b_im[l])
        a_t, wb, wc = _ssm_weights(abr, abi, bbar_r, bbar_i, ssm_c_re[l], ssm_c_im[l])
        wglu = w_glu[l].astype(BF16)
        w_ssm, w_att, w_mem = (w_out_ssm[l].astype(BF16), w_out_att[l].astype(BF16),
                               w_out_mem[l].astype(BF16))
        wo = w_o[l].astype(BF16)
        last = l == depth - 1

        h, ikw = _project(xp, rms_g[l], w_main, w_ikw, tabs_p, tabs_pk, tm=1024)
        a_ssm, st = ssm_branch(h, _state_to_tiles(zero_state, zero_state), a_t, wb, wc, ssm_d[l], wglu,
                               nb=Bp, tc=64)
        a_att = dsa_prompt(h, ikw, nb=Bp, tq=128)
        mkv = norm_matmul(mem_rows, mem_norm[l], w_mem_kv[l].astype(BF16), (one_tab,) * 4,
                          tm=Bp * N_MEM, tn=512)
        a_mem = mem_attn_prompt(h, mkv, nb=Bp, tq=512)
        merged = merge_branches(a_ssm, a_att, a_mem, h, w_ssm, w_att, w_mem, tm=512)
        xp = out_proj(merged, wo, xp, final_norm, tm=512, final_norm=last)
        sr, si = _tiles_to_state(st, Bp)
        kp_l.append(h[:, OFF_K:OFF_K + D_KV].reshape(Bp, T, N_KV_HEADS, HEAD_DIM))
        vp_l.append(h[:, OFF_V:OFF_V + D_KV].reshape(Bp, T, N_KV_HEADS, HEAD_DIM))
        ikp_l.append(ikw[:, :IDX_DIM].reshape(Bp, T, IDX_DIM))
        mkp_l.append(mkv[:, :D_MEM].reshape(Bp, N_MEM, MEM_HEADS, MEM_HEAD_DIM))
        mvp_l.append(mkv[:, D_MEM:].reshape(Bp, N_MEM, MEM_HEADS, MEM_HEAD_DIM))
        srp_l.append(sr)
        sip_l.append(si)

        h, ikw = _project(xs, rms_g[l], w_main, w_ikw, tabs_s, tabs_sk, tm=DS * DB)
        a_ssm, st = ssm_branch(h, _state_to_tiles(state_ssm_re[l], state_ssm_im[l]), a_t, wb, wc,
                               ssm_d[l], wglu, nb=DB, tc=DS)
        hb = h.reshape(DB, DS, N_MAIN)
        ikwb = ikw.reshape(DB, DS, LANES)
        iqm = hb[:, :, OFF_IQ:OFF_IQ + IDX_HEADS * IDX_DIM].reshape(DB, DS, IDX_HEADS, IDX_DIM)
        iqm = jnp.concatenate([iqm.transpose(0, 2, 1, 3)] * GQA, axis=2)
        iqm = jnp.pad(iqm.reshape(DB, IDX_HEADS * SUBLANES, IDX_DIM), ((0, 0), (0, 0), (0, LANES - IDX_DIM)))
        iw = ikwb[:, :, IDX_DIM:IDX_DIM + IDX_HEADS] * IW_SCALE
        iwb = jnp.concatenate([iw.transpose(0, 2, 1)] * GQA, axis=2).reshape(DB, IDX_HEADS * SUBLANES, 1)
        iwb = jnp.broadcast_to(iwb, (DB, IDX_HEADS * SUBLANES, LANES))
        iknew = jnp.pad(ikwb[:, :, :IDX_DIM].transpose(0, 2, 1), ((0, 0), (0, 0), (0, PAGE_SIZE - DS)))
        kv_pad = ((0, 0), (0, (PAGE_SIZE - DS) * N_KV_HEADS), (0, 0))
        knew = jnp.pad(hb[:, :, OFF_K:OFF_K + D_KV].reshape(DB, DS * N_KV_HEADS, HEAD_DIM), kv_pad)
        vnew = jnp.pad(hb[:, :, OFF_V:OFF_V + D_KV].reshape(DB, DS * N_KV_HEADS, HEAD_DIM), kv_pad)
        qm = hb[:, :, OFF_Q:OFF_Q + D_ATT].reshape(DB, DS, N_KV_HEADS, GQA, HEAD_DIM)
        qm = qm.transpose(0, 2, 3, 1, 4).reshape(DB, N_KV_HEADS * GQA * DS, HEAD_DIM).astype(BF16)
        scores = idx_scores_sample(kidx_t, l, pt_flat, iqm, iwb, n_pages=n_pages, pp=pp)
        o = attn_sample(cache_kr, cache_vr, l, pt_flat, scores, iqm, iwb, iknew, qm, knew, vnew,
                        n_pages=n_pages, pp=pp, ds=DS)
        y_att = o.reshape(DB, N_KV_HEADS, GQA, DS, HEAD_DIM).transpose(0, 3, 1, 2, 4).reshape(DB * DS, D_ATT)
        a_att = silu_gate(y_att, h[:, OFF_ZA:OFF_ZA + D_ATT])
        tpad = ((0, 0), (0, SUBLANES - DS), (0, 0))
        a_mem = mem_attn_sample(jnp.pad(hb[:, :, OFF_MQ:OFF_MQ + D_MEM], tpad),
                                jnp.pad(hb[:, :, OFF_ZM:OFF_ZM + D_MEM], tpad),
                                cache_mem_k[l].reshape(DB, N_MEM, D_MEM),
                                cache_mem_v[l].reshape(DB, N_MEM, D_MEM))
        a_mem = a_mem[:, :DS].reshape(DB * DS, D_MEM)
        merged = merge_branches(a_ssm, a_att, a_mem, h, w_ssm, w_att, w_mem, tm=DS * DB)
        xs = out_proj(merged, wo, xs, final_norm, tm=DS * DB, final_norm=last)
        sr, si = _tiles_to_state(st, DB)
        ks_l.append(hb[:, :, OFF_K:OFF_K + D_KV].reshape(DB, DS, N_KV_HEADS, HEAD_DIM))
        vs_l.append(hb[:, :, OFF_V:OFF_V + D_KV].reshape(DB, DS, N_KV_HEADS, HEAD_DIM))
        iks_l.append(ikwb[:, :, :IDX_DIM])
        srs_l.append(sr)
        sis_l.append(si)

    y_prompt = xp.reshape(Bp, T, D_MODEL)
    y_sample = xs.reshape(DB, DS, D_MODEL)
    return (y_prompt, y_sample,
            jnp.stack(kp_l), jnp.stack(vp_l), jnp.stack(ikp_l),
            jnp.stack(mkp_l), jnp.stack(mvp_l), jnp.stack(srp_l), jnp.stack(sip_l),
            jnp.stack(ks_l), jnp.stack(vs_l), jnp.stack(iks_l), jnp.stack(srs_l), jnp.stack(sis_l))
```
